```python
import jax, jax.numpy as jnp
from jax import lax
import numpy as np

D_MODEL = 2048
BATCH = 4
SEQ = 4096
DEPTH = 4
DEC_BATCH = 32
DEC_SEQ = 32
PAST_LEN = 4096

CHUNK = 64
GM_CHUNK = 128
GM_GROUPS = 8
GM_WIDTH = 1024
GM_GROUP_DIM = GM_WIDTH // GM_GROUPS
LRU_WIDTH = 1024
LRU_BLOCKS = 8
LRU_BLOCK_DIM = LRU_WIDTH // LRU_BLOCKS
LRU_CONV = 4
LRU_C = 8.0
HG_HEADS = 8
HG_DK = 128
HG_DV = 128
HG_WIDTH = HG_HEADS * HG_DV
N_BRANCH = 3
MIX_WIDTH = GM_WIDTH + LRU_WIDTH + HG_WIDTH
IN_SPLITS = (GM_WIDTH, GM_WIDTH, GM_WIDTH,
             LRU_WIDTH, LRU_WIDTH,
             HG_HEADS * HG_DK, HG_HEADS * HG_DK,
             HG_WIDTH, HG_WIDTH,
             N_BRANCH * D_MODEL)
IN_WIDTH = sum(IN_SPLITS)
EPS = 1e-6

kernel_name = "hybrid_gmlp_rglru_hgrn2_stream_step"


def _split_points():
    pts, acc = [], 0
    for s in IN_SPLITS[:-1]:
        acc += s
        pts.append(acc)
    return pts


def rmsnorm(x, g):
    xf = x.astype(jnp.float32)
    y = xf * lax.rsqrt(jnp.mean(xf * xf, axis=-1, keepdims=True) + EPS)
    return (y * g.astype(jnp.float32)).astype(x.dtype)


def gmlp_mix(u, v, ws, bs):
    b, t, _ = v.shape
    L = min(t, GM_CHUNK)
    n = t // L
    mask = jnp.tril(jnp.ones((L, L), dtype=bool))
    w = jnp.where(mask[None], ws[:, :L, :L], 0.0)
    vg = v.reshape(b, n, L, GM_GROUPS, GM_GROUP_DIM)
    s = jnp.einsum('gts,bnsgc->bntgc', w, vg) + bs[:, :L].T[None, None, :, :, None]
    return u * s.reshape(b, t, GM_WIDTH)


def causal_conv(x, prev, w, bias):
    t = x.shape[1]
    xp = jnp.concatenate([prev.astype(x.dtype), x], axis=1)
    y = bias
    for j in range(LRU_CONV):
        y = y + w[j] * xp[:, j:j + t]
    return y, xp[:, -(LRU_CONV - 1):]


def block_diag(x, w, bias):
    b, t, _ = x.shape
    xb = x.reshape(b, t, LRU_BLOCKS, LRU_BLOCK_DIM)
    return jnp.einsum('btnd,nde->btne', xb, w).reshape(b, t, LRU_WIDTH) + bias


def rglru(x, h0, w_a, b_a, w_x, b_x, lam):
    f32 = jnp.float32
    gate_r = jax.nn.sigmoid(block_diag(x, w_a, b_a).astype(f32))
    gate_i = jax.nn.sigmoid(block_diag(x, w_x, b_x).astype(f32))
    log_a = -LRU_C * gate_r * jax.nn.softplus(-lam.astype(f32))
    a = jnp.exp(log_a)
    drive = jnp.sqrt(-jnp.expm1(2.0 * log_a)) * gate_i * x.astype(f32)
    drive = drive.at[:, 0].add(a[:, 0] * h0.astype(f32))

    def combine(lhs, rhs):
        al, bl = lhs
        ar, br = rhs
        return al * ar, ar * bl + br

    _, h = lax.associative_scan(combine, (a, drive), axis=1)
    return h.astype(x.dtype), h[:, -1].astype(h0.dtype)


def hgrn2_chunk(S, q, k, g, v):
    L = q.shape[1]
    G = jnp.cumsum(g, axis=1)
    o_inter = jnp.einsum('blhk,bhkv->blhv', q * jnp.exp(G), S)
    mask = jnp.tril(jnp.ones((L, L), dtype=bool))
    diff = G[:, :, None] - G[:, None, :]
    decay = jnp.exp(jnp.where(mask[None, :, :, None, None], diff, -jnp.inf))
    A = jnp.einsum('bthk,btshk,bshk->bhts', q, decay, k)
    o = o_inter + jnp.einsum('bhts,bshv->bthv', A, v)
    G_last = G[:, -1]
    k_dec = k * jnp.exp(G_last[:, None] - G)
    S_new = jnp.exp(G_last)[..., None] * S + jnp.einsum('blhk,blhv->bhkv', k_dec, v)
    return S_new, o


def hgrn2_scan(S0, q, k, g, v):
    b, t = q.shape[:2]
    L = min(t, CHUNK)
    n = t // L

    def to_chunks(a):
        return a.reshape(b, n, L, *a.shape[2:]).swapaxes(0, 1)

    S, o = lax.scan(lambda s, c: hgrn2_chunk(s, *c), S0.astype(jnp.float32),
                    (to_chunks(q), to_chunks(k), to_chunks(g), to_chunks(v)))
    return o.swapaxes(0, 1).reshape(b, t, HG_HEADS, HG_DV), S.astype(S0.dtype)


def hgrn2_lower_bounds(lb_param):
    p = jax.nn.softmax(lb_param.astype(jnp.float32), axis=0)
    cs = jnp.cumsum(p, axis=0)
    return cs - cs[0]


def mixer_layer(x, c, conv_prev, h0, S0, lb, w_ada, b_ada, norm_g, w_in, gm_vnorm_g, gm_ws, gm_bs,
                lru_conv_w, lru_conv_b, lru_wa, lru_ba, lru_wx, lru_bx, lru_lambda, hg_onorm_g,
                w_branch, w_out):
    b, t, _ = x.shape
    f32 = jnp.float32
    mod = jax.nn.silu(c) @ w_ada + b_ada
    shift, scale, gate = jnp.split(mod[:, None, :], 3, axis=-1)
    h = rmsnorm(x, norm_g) * (1.0 + scale) + shift
    z = h @ w_in
    au, av, ag, lx, lg, hq, hf, hi, hg, mg = jnp.split(z, _split_points(), axis=-1)

    au = jax.nn.gelu(au)
    av = rmsnorm(jax.nn.gelu(av), gm_vnorm_g)
    ya = gmlp_mix(au, av, gm_ws, gm_bs) * jax.nn.silu(ag)

    lx, conv_new = causal_conv(lx, conv_prev, lru_conv_w, lru_conv_b)
    hb, h_new = rglru(lx, h0, lru_wa, lru_ba, lru_wx, lru_bx, lru_lambda)
    yb = hb * jax.nn.silu(lg)

    q = jax.nn.silu(hq.astype(f32)).reshape(b, t, HG_HEADS, HG_DK)
    zf = hf.astype(f32).reshape(b, t, HG_HEADS, HG_DK)
    lbh = lb.reshape(HG_HEADS, HG_DK)
    log_f = jnp.logaddexp(jnp.log(lbh), jnp.log1p(-lbh) + jax.nn.log_sigmoid(zf))
    k = (1.0 - lbh) * jax.nn.sigmoid(-zf)
    v = hi.astype(f32).reshape(b, t, HG_HEADS, HG_DV)
    o, S_new = hgrn2_scan(S0, q, k, log_f, v)
    yc = rmsnorm(o, hg_onorm_g).astype(x.dtype).reshape(b, t, HG_WIDTH) * jax.nn.silu(hg)

    pa = ya @ w_branch[:GM_WIDTH]
    pb = yb @ w_branch[GM_WIDTH:GM_WIDTH + LRU_WIDTH]
    pc = yc @ w_branch[GM_WIDTH + LRU_WIDTH:]
    ga, gb, gc = jnp.split(jax.nn.sigmoid(mg), N_BRANCH, axis=-1)
    out = (ga * pa + gb * pb + gc * pc) @ w_out
    return x + gate * out, av, conv_new, h_new, S_new


def setup_inputs(seed: int = 0) -> dict:
    key = jax.random.key(seed)
    ks = jax.random.split(key, 32)
    nrm = lambda k, shape, s: jax.random.normal(k, shape, jnp.float32) * s
    a_c = 0.9 + 0.099 * jax.random.uniform(ks[20], (DEPTH, LRU_WIDTH), jnp.float32)
    a0 = a_c ** (1.0 / LRU_C)
    return {
        "x_prompt": nrm(ks[0], (BATCH, SEQ, D_MODEL), 1.0),
        "x_sample": nrm(ks[1], (DEC_BATCH, DEC_SEQ, D_MODEL), 1.0),
        "c_prompt": nrm(ks[2], (BATCH, D_MODEL), 1.0),
        "c_sample": nrm(ks[3], (DEC_BATCH, D_MODEL), 1.0),
        "state_rglru_conv": nrm(ks[4], (DEPTH, DEC_BATCH, LRU_CONV - 1, LRU_WIDTH), 1.0),
        "state_rglru_h": nrm(ks[5], (DEPTH, DEC_BATCH, LRU_WIDTH), 0.5),
        "state_hgrn2": nrm(ks[6], (DEPTH, DEC_BATCH, HG_HEADS, HG_DK, HG_DV), 0.3),
        "w_ada": nrm(ks[7], (DEPTH, D_MODEL, 3 * D_MODEL), D_MODEL ** -0.5),
        "b_ada": nrm(ks[8], (DEPTH, 3 * D_MODEL), 0.01),
        "norm_g": 1.0 + nrm(ks[9], (DEPTH, D_MODEL), 0.01),
        "w_in": nrm(ks[10], (DEPTH, D_MODEL, IN_WIDTH), D_MODEL ** -0.5),
        "gm_vnorm_g": 1.0 + nrm(ks[11], (DEPTH, GM_WIDTH), 0.01),
        "gm_ws": nrm(ks[12], (DEPTH, GM_GROUPS, GM_CHUNK, GM_CHUNK), GM_CHUNK ** -0.5),
        "gm_bs": 1.0 + nrm(ks[13], (DEPTH, GM_GROUPS, GM_CHUNK), 0.01),
        "lru_conv_w": nrm(ks[14], (DEPTH, LRU_CONV, LRU_WIDTH), LRU_CONV ** -0.5),
        "lru_conv_b": nrm(ks[15], (DEPTH, LRU_WIDTH), 0.01),
        "lru_wa": nrm(ks[16], (DEPTH, LRU_BLOCKS, LRU_BLOCK_DIM, LRU_BLOCK_DIM), LRU_BLOCK_DIM ** -0.5),
        "lru_ba": nrm(ks[17], (DEPTH, LRU_WIDTH), 0.01),
        "lru_wx": nrm(ks[18], (DEPTH, LRU_BLOCKS, LRU_BLOCK_DIM, LRU_BLOCK_DIM), LRU_BLOCK_DIM ** -0.5),
        "lru_bx": nrm(ks[19], (DEPTH, LRU_WIDTH), 0.01),
        "lru_lambda": jnp.log(a0) - jnp.log1p(-a0),
        "hg_lb": nrm(ks[21], (DEPTH, HG_HEADS * HG_DK), 0.1),
        "hg_onorm_g": 1.0 + nrm(ks[22], (DEPTH, HG_DV), 0.01),
        "w_branch": nrm(ks[23], (DEPTH, MIX_WIDTH, D_MODEL), GM_WIDTH ** -0.5),
        "w_out": nrm(ks[24], (DEPTH, D_MODEL, D_MODEL), D_MODEL ** -0.5),
        "final_g": 1.0 + nrm(ks[25], (D_MODEL,), 0.01),
    }


def reference(x_prompt, x_sample, c_prompt, c_sample, state_rglru_conv, state_rglru_h, state_hgrn2,
              w_ada, b_ada, norm_g, w_in, gm_vnorm_g, gm_ws, gm_bs, lru_conv_w, lru_conv_b,
              lru_wa, lru_ba, lru_wx, lru_bx, lru_lambda, hg_lb, hg_onorm_g, w_branch, w_out, final_g):
    lbs = hgrn2_lower_bounds(hg_lb)
    dt = x_prompt.dtype
    xp, xs = x_prompt, x_sample
    conv_p0 = jnp.zeros((BATCH, LRU_CONV - 1, LRU_WIDTH), dt)
    h_p0 = jnp.zeros((BATCH, LRU_WIDTH), dt)
    S_p0 = jnp.zeros((BATCH, HG_HEADS, HG_DK, HG_DV), dt)
    conv_p, h_p, S_p, conv_s, h_s, S_s, v_s = [], [], [], [], [], [], []
    for l in range(DEPTH):
        w = (w_ada[l], b_ada[l], norm_g[l], w_in[l], gm_vnorm_g[l], gm_ws[l], gm_bs[l],
             lru_conv_w[l], lru_conv_b[l], lru_wa[l], lru_ba[l], lru_wx[l], lru_bx[l],
             lru_lambda[l], hg_onorm_g[l], w_branch[l], w_out[l])
        xp, _, cp, hp, sp = mixer_layer(xp, c_prompt, conv_p0, h_p0, S_p0, lbs[l], *w)
        xs, vs, cs, hs, ss = mixer_layer(xs, c_sample, state_rglru_conv[l], state_rglru_h[l],
                                         state_hgrn2[l], lbs[l], *w)
        conv_p.append(cp); h_p.append(hp); S_p.append(sp)
        conv_s.append(cs); h_s.append(hs); S_s.append(ss); v_s.append(vs)
    y_prompt = rmsnorm(xp, final_g)
    y_sample = rmsnorm(xs, final_g)
    return (y_prompt, y_sample, jnp.stack(conv_p), jnp.stack(h_p), jnp.stack(S_p),
            jnp.stack(conv_s), jnp.stack(h_s), jnp.stack(S_s), jnp.stack(v_s))
```

```python
import functools

import jax
import jax.numpy as jnp
from jax import lax
from jax.experimental import pallas as pl
from jax.experimental.pallas import tpu as pltpu

F32 = jnp.float32
BF16 = jnp.bfloat16

D_MODEL = 2048
DEPTH = 4
GM_CHUNK = 128
GM_GROUPS = 8
GM_WIDTH = 1024
LRU_WIDTH = 1024
LRU_BLOCKS = 8
LRU_CONV = 4
LRU_C = 8.0
HG_CHUNK = 64
HG_HEADS = 8
HG_DK = 128
HG_DV = 128
HG_WIDTH = HG_HEADS * HG_DV
N_BRANCH = 3
EPS = 1e-6

LANES = 128
SUBLANES = 8
COL = 1024
COL_GM_U, COL_GM_V, COL_GM_G = 0, 1, 2
COL_LRU_X, COL_LRU_G = 3, 4
COL_HG_Q, COL_HG_F, COL_HG_I, COL_HG_O = 5, 6, 7, 8
COL_MERGE = 9
TILE_ROWS = 256
HG_SUB = 16
HG_CLAMP = 60.0
V7X_VMEM_BYTES = 64 * 1024 * 1024


class _Group:
    def __init__(self, batch, seq):
        self.batch, self.seq = batch, seq
        self.ts = min(seq, TILE_ROWS)
        self.nb = TILE_ROWS // self.ts
        assert batch % self.nb == 0 and seq % self.ts == 0
        self.tm = self.nb * self.ts
        self.grid = (batch // self.nb, seq // self.ts)
        self.gm_chunk = min(seq, GM_CHUNK)
        self.hg_chunk = min(seq, HG_CHUNK)
        assert self.ts % self.gm_chunk == 0 and self.ts % self.hg_chunk == 0
        assert self.hg_chunk % HG_SUB == 0
        assert self.nb == 1 or self.grid[1] == 1


def _dot(a, b):
    return jnp.dot(a, b, preferred_element_type=F32)


def _rmsnorm(x, g):
    return x * lax.rsqrt(jnp.mean(x * x, axis=-1, keepdims=True) + EPS) * g


def _tok_spec(grp, width):
    return pl.BlockSpec((grp.nb, grp.ts, width), lambda b, j: (b, j, 0))


def _seq_spec(grp, *tail):
    zeros = (0,) * len(tail)
    return pl.BlockSpec((grp.nb,) + tail, lambda b, j: (b,) + zeros)


def _const_spec(shape):
    nd = len(shape)
    return pl.BlockSpec(shape, lambda b, j: (0,) * nd, pipeline_mode=pl.Buffered(1))


def _layer_spec(layer, *tail):
    zeros = (0,) * len(tail)
    return pl.BlockSpec((None,) + tail, lambda b, j: (layer,) + zeros, pipeline_mode=pl.Buffered(1))


def _win_spec(layer, col):
    return pl.BlockSpec((None, D_MODEL, COL), lambda b, j: (layer, 0, col), pipeline_mode=pl.Buffered(1))


def _wbr_spec(layer, branch):
    return pl.BlockSpec((None, COL, D_MODEL), lambda b, j: (layer, branch, 0), pipeline_mode=pl.Buffered(1))


def _params(vmem_bytes):
    return pltpu.CompilerParams(dimension_semantics=("arbitrary", "arbitrary"),
                                vmem_limit_bytes=min(int(vmem_bytes), V7X_VMEM_BYTES - (4 << 20)))


def _gated_out(hh, y, mg0_ref, mg1_ref, wbr_ref, out_ref, grp):
    for half, mg_ref in enumerate((mg0_ref, mg1_ref)):
        cols = slice(half * COL, (half + 1) * COL)
        g = jax.nn.sigmoid(_dot(hh, mg_ref[...]))
        p = _dot(y, wbr_ref[:, cols])
        out_ref[:, :, cols] = (g * p).reshape(grp.nb, grp.ts, COL)


def _segment_scan(a, b, seg):
    rows = b.shape[0]
    pos = lax.broadcasted_iota(jnp.int32, (rows, 1), 0) % seg
    d = 1
    while d < seg:
        keep = pos >= d
        b_prev = pltpu.roll(b, d, 0)
        if a is None:
            b = jnp.where(keep, b + b_prev, b)
        else:
            b = jnp.where(keep, a * b_prev + b, b)
            if 2 * d < seg:
                a = jnp.where(keep, a * pltpu.roll(a, d, 0), a)
        d *= 2
    return b


def _mod_kernel(c_ref, w_ref, b_ref, o_ref):
    sc = jax.nn.silu(c_ref[...]).astype(BF16)
    o_ref[...] = _dot(sc, w_ref[...].astype(BF16)) + b_ref[...]


def _mod_call(c_all, w_ada, b_ada):
    rows = c_all.shape[0]
    width = 3 * D_MODEL
    tn = 1024
    return pl.pallas_call(
        _mod_kernel,
        grid=(DEPTH, width // tn),
        in_specs=[pl.BlockSpec((rows, D_MODEL), lambda l, n: (0, 0)),
                  pl.BlockSpec((None, D_MODEL, tn), lambda l, n: (l, 0, n)),
                  pl.BlockSpec((None, 1, tn), lambda l, n: (l, 0, n))],
        out_specs=pl.BlockSpec((None, rows, tn), lambda l, n: (l, 0, n)),
        out_shape=jax.ShapeDtypeStruct((DEPTH, rows, width), F32),
        compiler_params=_params(2 * D_MODEL * tn * 4 + D_MODEL * tn * 2 + (8 << 20)),
        name="adaln_mod",
    )(c_all, w_ada, b_ada.reshape(DEPTH, 1, width))


def _adaln(x, g, scale, shift):
    return (_rmsnorm(x, g) * (1.0 + scale) + shift).astype(BF16)


def _prenorm_kernel(x_ref, g_ref, scale_ref, shift_ref, h_ref):
    h_ref[...] = _adaln(x_ref[...], g_ref[...], scale_ref[...], shift_ref[...])


def _prenorm_call(grp, layer, x, norm_g, scale, shift):
    return pl.pallas_call(
        _prenorm_kernel,
        grid=grp.grid,
        in_specs=[_tok_spec(grp, D_MODEL), _layer_spec(layer, 1, D_MODEL),
                  _seq_spec(grp, 1, D_MODEL), _seq_spec(grp, 1, D_MODEL)],
        out_specs=_tok_spec(grp, D_MODEL),
        out_shape=jax.ShapeDtypeStruct(x.shape, BF16),
        compiler_params=_params(grp.tm * D_MODEL * (2 * 4 + 2 * 2 + 3 * 4) + (4 << 20)),
        name="prenorm",
    )(x, norm_g, scale, shift)


def _gmlp_kernel(h_ref, wu_ref, wv_ref, wg_ref, mg0_ref, mg1_ref, wbr_ref, vg_ref, wbd_ref, bias_ref,
                 out_ref, *v_out, grp):
    hh = h_ref[...].reshape(grp.tm, D_MODEL)
    v = _rmsnorm(jax.nn.gelu(_dot(hh, wv_ref[...])), vg_ref[...])
    if v_out:
        v_out[0][...] = v.reshape(grp.nb, grp.ts, GM_WIDTH)
    vb = v.astype(BF16)
    width = GM_WIDTH // GM_GROUPS
    s = jnp.concatenate([_dot(wbd_ref[g], vb[:, g * width:(g + 1) * width]) for g in range(GM_GROUPS)],
                        axis=-1) + bias_ref[...]
    u = jax.nn.gelu(_dot(hh, wu_ref[...]))
    ya = (u * s) * jax.nn.silu(_dot(hh, wg_ref[...]))
    _gated_out(hh, ya.astype(BF16), mg0_ref, mg1_ref, wbr_ref, out_ref, grp)


def _gmlp_call(grp, layer, h, w_in, w_branch, vnorm_g, wbd, bias, want_v):
    tm = grp.tm
    out_shape = [jax.ShapeDtypeStruct((grp.batch, grp.seq, D_MODEL), F32)]
    out_specs = [_tok_spec(grp, D_MODEL)]
    if want_v:
        out_shape.append(jax.ShapeDtypeStruct((grp.batch, grp.seq, GM_WIDTH), F32))
        out_specs.append(_tok_spec(grp, GM_WIDTH))
    resident = (5 * D_MODEL * COL + COL * D_MODEL + GM_GROUPS * tm * tm) * 2 + tm * GM_WIDTH * 4
    streamed = 2 * tm * (D_MODEL * 2 + D_MODEL * 4 + GM_WIDTH * 4)
    temps = 10 * tm * GM_WIDTH * 4
    return pl.pallas_call(
        functools.partial(_gmlp_kernel, grp=grp),
        grid=grp.grid,
        in_specs=[_tok_spec(grp, D_MODEL),
                  _win_spec(layer, COL_GM_U), _win_spec(layer, COL_GM_V), _win_spec(layer, COL_GM_G),
                  _win_spec(layer, COL_MERGE), _win_spec(layer, COL_MERGE + 1), _wbr_spec(layer, 0),
                  _layer_spec(layer, 1, GM_WIDTH), _const_spec(wbd.shape), _const_spec(bias.shape)],
        out_specs=out_specs,
        out_shape=out_shape,
        compiler_params=_params(resident + streamed + temps),
        name="gmlp_branch",
    )(h, w_in, w_in, w_in, w_in, w_in, w_branch, vnorm_g, wbd, bias)


CONV_PAD = 8


def _rglru_kernel(h_ref, wx_ref, wg_ref, mg0_ref, mg1_ref, wbr_ref, convw_ref, convb_ref, wa_ref, ba_ref,
                  wi_ref, bi_ref, sp_ref, conv0_ref, h0_ref,
                  out_ref, convn_ref, hn_ref, xbuf, hcar, *, grp):
    nb, ts, tm = grp.nb, grp.ts, grp.tm
    hist = CONV_PAD - (LRU_CONV - 1)

    @pl.when(pl.program_id(1) == 0)
    def _():
        xbuf[:, hist:CONV_PAD, :] = conv0_ref[...]
        hcar[...] = h0_ref[...]

    hh = h_ref[...].reshape(tm, D_MODEL)
    xbuf[:, CONV_PAD:CONV_PAD + ts, :] = _dot(hh, wx_ref[...]).reshape(nb, ts, LRU_WIDTH)
    y = convb_ref[...]
    for tap in range(LRU_CONV):
        y = y + convw_ref[tap:tap + 1, :] * xbuf[:, hist + tap:hist + tap + ts, :]
    conv_new = xbuf[:, CONV_PAD + ts - (LRU_CONV - 1):CONV_PAD + ts, :]
    xbuf[:, hist:CONV_PAD, :] = conv_new
    convn_ref[...] = conv_new

    x = y.reshape(tm, LRU_WIDTH)
    xb = x.astype(BF16)
    width = LRU_WIDTH // LRU_BLOCKS

    def block_diag(w_ref, b_ref):
        return jnp.concatenate([_dot(xb[:, n * width:(n + 1) * width], w_ref[n]) for n in range(LRU_BLOCKS)],
                               axis=-1) + b_ref[...]

    gate_r = jax.nn.sigmoid(block_diag(wa_ref, ba_ref))
    gate_i = jax.nn.sigmoid(block_diag(wi_ref, bi_ref))
    log_a = -LRU_C * gate_r * sp_ref[...]
    a = jnp.exp(log_a)
    drive = jnp.sqrt(-jnp.tanh(log_a) * (a * a + 1.0)) * gate_i * x
    h0 = jnp.broadcast_to(hcar[...], (nb, ts, LRU_WIDTH)).reshape(tm, LRU_WIDTH)
    first = lax.broadcasted_iota(jnp.int32, (tm, 1), 0) % ts == 0
    drive = drive + jnp.where(first, a * h0, 0.0)
    hseq = _segment_scan(a, drive, ts)
    h_last = hseq.reshape(nb, ts, LRU_WIDTH)[:, ts - 1:ts, :]
    hcar[...] = h_last
    hn_ref[...] = h_last
    yb = hseq * jax.nn.silu(_dot(hh, wg_ref[...]))
    _gated_out(hh, yb.astype(BF16), mg0_ref, mg1_ref, wbr_ref, out_ref, grp)


def _rglru_call(grp, layer, h, w_in, w_branch, conv_w, conv_b, wa, ba, wi, bi, sp, conv0, h0):
    tm = grp.tm
    resident = (4 * D_MODEL * COL + COL * D_MODEL + 2 * LRU_WIDTH * LRU_WIDTH // LRU_BLOCKS) * 2
    streamed = 2 * tm * (D_MODEL * 2 + D_MODEL * 4)
    temps = 12 * tm * LRU_WIDTH * 4
    return pl.pallas_call(
        functools.partial(_rglru_kernel, grp=grp),
        grid=grp.grid,
        in_specs=[_tok_spec(grp, D_MODEL),
                  _win_spec(layer, COL_LRU_X), _win_spec(layer, COL_LRU_G),
                  _win_spec(layer, COL_MERGE + 2), _win_spec(layer, COL_MERGE + 3), _wbr_spec(layer, 1),
                  _layer_spec(layer, LRU_CONV, LRU_WIDTH), _layer_spec(layer, 1, LRU_WIDTH),
                  _layer_spec(layer, LRU_BLOCKS, LANES, LANES), _layer_spec(layer, 1, LRU_WIDTH),
                  _layer_spec(layer, LRU_BLOCKS, LANES, LANES), _layer_spec(layer, 1, LRU_WIDTH),
                  _layer_spec(layer, 1, LRU_WIDTH),
                  _seq_spec(grp, LRU_CONV - 1, LRU_WIDTH), _seq_spec(grp, 1, LRU_WIDTH)],
        out_specs=[_tok_spec(grp, D_MODEL), _seq_spec(grp, LRU_CONV - 1, LRU_WIDTH), _seq_spec(grp, 1, LRU_WIDTH)],
        out_shape=[jax.ShapeDtypeStruct((grp.batch, grp.seq, D_MODEL), F32),
                   jax.ShapeDtypeStruct((grp.batch, LRU_CONV - 1, LRU_WIDTH), F32),
                   jax.ShapeDtypeStruct((grp.batch, 1, LRU_WIDTH), F32)],
        scratch_shapes=[pltpu.VMEM((grp.nb, CONV_PAD + grp.ts, LRU_WIDTH), F32),
                        pltpu.VMEM((grp.nb, 1, LRU_WIDTH), F32)],
        compiler_params=_params(resident + streamed + temps),
        name="rglru_branch",
    )(h, w_in, w_in, w_in, w_in, w_branch, conv_w, conv_b, wa, ba, wi, bi, sp, conv0, h0)


def _hgrn2_kernel(h_ref, wq_ref, wf_ref, wi_ref, wo_ref, mg0_ref, mg1_ref, wbr_ref,
                  loglb_ref, log1mlb_ref, omlb_ref, og_ref, s0_ref,
                  out_ref, sn_ref, q_scr, k_scr, v_scr, g_scr, inter_scr, intra_scr, *, grp):
    nb, tm, L = grp.nb, grp.tm, grp.hg_chunk
    n_chunks = tm // L
    n_sub = L // HG_SUB

    @pl.when(pl.program_id(1) == 0)
    def _():
        sn_ref[...] = s0_ref[...]

    hh = h_ref[...].reshape(tm, D_MODEL)
    q_scr[...] = jax.nn.silu(_dot(hh, wq_ref[...]))
    zf = _dot(hh, wf_ref[...])
    log_f = jnp.logaddexp(loglb_ref[...], log1mlb_ref[...] + jax.nn.log_sigmoid(zf))
    k_scr[...] = omlb_ref[...] * jax.nn.sigmoid(-zf)
    g_scr[...] = _segment_scan(None, log_f, L)
    v_scr[...] = _dot(hh, wi_ref[...])

    tril = lax.broadcasted_iota(jnp.int32, (L, L), 0) >= lax.broadcasted_iota(jnp.int32, (L, L), 1)

    def chunk_inputs(c, hd):
        rows = pl.ds(pl.multiple_of(c * L, L), L)
        cols = slice(hd * HG_DK, (hd + 1) * HG_DK)
        return rows, cols, g_scr[rows, cols], q_scr[rows, cols], k_scr[rows, cols], v_scr[rows, cols]

    def factored_chunk(c, decay_min):
        slot = 0 if nb == 1 else c
        for hd in range(HG_HEADS):
            rows, cols, G, q, k, v = chunk_inputs(c, hd)
            S = sn_ref[slot, hd]
            inter_scr[rows, cols] = _dot((q * jnp.exp(G)).astype(BF16), S.astype(BF16))
            a_rows = []
            for i in range(n_sub):
                lo = i * HG_SUB
                ref = jnp.zeros((1, HG_DK), F32) if i == 0 else G[lo - 1:lo, :]
                g_sub = G[lo:lo + HG_SUB, :] - ref
                decay_min = jnp.minimum(decay_min, g_sub[HG_SUB - 1:HG_SUB, :])
                qt = (q[lo:lo + HG_SUB, :] * jnp.exp(g_sub)).astype(BF16)
                kt = (k * jnp.exp(jnp.minimum(ref - G, HG_CLAMP))).astype(BF16)
                a_rows.append(lax.dot_general(qt, kt, (((1,), (1,)), ((), ())), preferred_element_type=F32))
            A = jnp.where(tril, jnp.concatenate(a_rows, axis=0), 0.0)
            vb = v.astype(BF16)
            intra_scr[rows, cols] = _dot(A.astype(BF16), vb)
            g_last = G[L - 1:L, :]
            k_dec = (k * jnp.exp(g_last - G)).astype(BF16)
            s_decay = jnp.transpose(jnp.broadcast_to(jnp.exp(g_last), (HG_DV, HG_DK)))
            sn_ref[slot, hd] = s_decay * S + lax.dot_general(
                k_dec, vb, (((0,), (0,)), ((), ())), preferred_element_type=F32)
        return decay_min

    decay_min = lax.fori_loop(0, n_chunks, factored_chunk, jnp.zeros((1, HG_DK), F32))

    @pl.when(jnp.min(decay_min) < -HG_CLAMP)
    def _():
        t_idx = lax.broadcasted_iota(jnp.int32, (L, 1), 0)

        def pairwise_chunk(c, carry):
            for hd in range(HG_HEADS):
                rows, cols, G, q, _, _ = chunk_inputs(c, hd)

                def add_sources(blk, acc):
                    src = pl.ds(pl.multiple_of(c * L + blk * SUBLANES, SUBLANES), SUBLANES)
                    g_blk, k_blk, v_blk = g_scr[src, cols], k_scr[src, cols], v_scr[src, cols]
                    for r in range(SUBLANES):
                        g_s, k_s, v_s = g_blk[r:r + 1, :], k_blk[r:r + 1, :], v_blk[r:r + 1, :]
                        w = jnp.sum(q * k_s * jnp.exp(jnp.minimum(G - g_s, 0.0)), axis=-1, keepdims=True)
                        acc = acc + jnp.where(t_idx >= blk * SUBLANES + r, w, 0.0) * v_s
                    return acc

                intra_scr[rows, cols] = lax.fori_loop(0, L // SUBLANES, add_sources, jnp.zeros((L, HG_DV), F32))
            return carry

        lax.fori_loop(0, n_chunks, pairwise_chunk, 0)

    o = inter_scr[...] + intra_scr[...]
    y = jnp.concatenate(
        [_rmsnorm(o[:, hd * HG_DV:(hd + 1) * HG_DV], og_ref[...]) for hd in range(HG_HEADS)], axis=-1)
    yc = y * jax.nn.silu(_dot(hh, wo_ref[...]))
    _gated_out(hh, yc.astype(BF16), mg0_ref, mg1_ref, wbr_ref, out_ref, grp)


def _hgrn2_call(grp, layer, h, w_in, w_branch, loglb, log1mlb, omlb, onorm_g, s0):
    tm = grp.tm
    state = grp.nb * HG_HEADS * HG_DK * HG_DV * 4
    resident = (6 * D_MODEL * COL + COL * D_MODEL) * 2
    streamed = 2 * tm * (D_MODEL * 2 + D_MODEL * 4) + 3 * state
    scratch = 6 * tm * HG_WIDTH * 4
    temps = 8 * tm * HG_WIDTH * 4
    tile = pltpu.VMEM((tm, HG_WIDTH), F32)
    state_in = pl.BlockSpec((grp.nb, HG_HEADS, HG_DK, HG_DV), lambda b, j: (b, 0, 0, 0),
                            pipeline_mode=pl.Buffered(1))
    return pl.pallas_call(
        functools.partial(_hgrn2_kernel, grp=grp),
        grid=grp.grid,
        in_specs=[_tok_spec(grp, D_MODEL),
                  _win_spec(layer, COL_HG_Q), _win_spec(layer, COL_HG_F), _win_spec(layer, COL_HG_I),
                  _win_spec(layer, COL_HG_O),
                  _win_spec(layer, COL_MERGE + 4), _win_spec(layer, COL_MERGE + 5), _wbr_spec(layer, 2),
                  _layer_spec(layer, 1, HG_WIDTH), _layer_spec(layer, 1, HG_WIDTH), _layer_spec(layer, 1, HG_WIDTH),
                  _layer_spec(layer, 1, HG_DV), state_in],
        out_specs=[_tok_spec(grp, D_MODEL), _seq_spec(grp, HG_HEADS, HG_DK, HG_DV)],
        out_shape=[jax.ShapeDtypeStruct((grp.batch, grp.seq, D_MODEL), F32),
                   jax.ShapeDtypeStruct((grp.batch, HG_HEADS, HG_DK, HG_DV), F32)],
        scratch_shapes=[tile, tile, tile, tile, tile, tile],
        compiler_params=_params(resident + streamed + scratch + temps),
        name="hgrn2_branch",
    )(h, w_in, w_in, w_in, w_in, w_in, w_in, w_branch, loglb, log1mlb, omlb, onorm_g, s0)


def _outproj_kernel(oa_ref, ob_ref, oc_ref, x_ref, wout_ref, gate_ref, g_ref, *rest, grp, last):
    merged = (oa_ref[...] + ob_ref[...]) + oc_ref[...]
    out = _dot(merged.reshape(grp.tm, D_MODEL).astype(BF16), wout_ref[...]).reshape(grp.nb, grp.ts, D_MODEL)
    xn = x_ref[...] + gate_ref[...] * out
    if last:
        (y_ref,) = rest
        y_ref[...] = _rmsnorm(xn, g_ref[...])
    else:
        scale_ref, shift_ref, xn_ref, h_ref = rest
        xn_ref[...] = xn
        h_ref[...] = _adaln(xn, g_ref[...], scale_ref[...], shift_ref[...])


def _outproj_call(grp, layer, oa, ob, oc, x, w_out, gate, norm_g, scale=None, shift=None):
    last = scale is None
    tm = grp.tm
    tok = _tok_spec(grp, D_MODEL)
    in_specs = [tok, tok, tok, tok, _layer_spec(layer, D_MODEL, D_MODEL), _seq_spec(grp, 1, D_MODEL)]
    args = [oa, ob, oc, x, w_out, gate]
    if last:
        in_specs.append(pl.BlockSpec((1, D_MODEL), lambda b, j: (0, 0)))
        args.append(norm_g)
        out_specs, out_shape = tok, jax.ShapeDtypeStruct(x.shape, F32)
    else:
        in_specs += [_layer_spec(layer + 1, 1, D_MODEL), _seq_spec(grp, 1, D_MODEL), _seq_spec(grp, 1, D_MODEL)]
        args += [norm_g, scale, shift]
        out_specs = [tok, tok]
        out_shape = [jax.ShapeDtypeStruct(x.shape, F32), jax.ShapeDtypeStruct(x.shape, BF16)]
    resident = D_MODEL * D_MODEL * 2
    streamed = 2 * tm * D_MODEL * (5 * 4 + 2)
    temps = 4 * tm * D_MODEL * 4
    return pl.pallas_call(
        functools.partial(_outproj_kernel, grp=grp, last=last),
        grid=grp.grid,
        in_specs=in_specs,
        out_specs=out_specs,
        out_shape=out_shape,
        compiler_params=_params(resident + streamed + temps),
        name="outproj",
    )(*args)


def _gmlp_mix_weights(grp, ws, bs):
    L = grp.gm_chunk
    reps = grp.tm // L
    w = jnp.where(jnp.tril(jnp.ones((L, L), dtype=bool))[None], ws[:, :L, :L], 0.0)
    wbd = jnp.einsum('ab,gts->gatbs', jnp.eye(reps, dtype=F32), w).reshape(GM_GROUPS, grp.tm, grp.tm)
    bias = jnp.repeat(jnp.tile(bs[:, :L].T, (reps, 1)), GM_WIDTH // GM_GROUPS, axis=1)
    return wbd.astype(BF16), bias


def _run_group(grp, x, mod, conv0, h0, s0, want_v, weights):
    (norm_g, w_in, gm_vnorm_g, gm_ws, gm_bs, conv_w, conv_b, wa, ba, wi, bi, sp,
     loglb, log1mlb, omlb, onorm_g, w_branch, w_out, final_g) = weights
    shift = mod[:, :, None, 0:D_MODEL]
    scale = mod[:, :, None, D_MODEL:2 * D_MODEL]
    gate = mod[:, :, None, 2 * D_MODEL:]
    h = _prenorm_call(grp, 0, x, norm_g, scale[0], shift[0])
    convs, hs, ss, vs = [], [], [], []
    y = None
    for l in range(DEPTH):
        wbd, bias = _gmlp_mix_weights(grp, gm_ws[l], gm_bs[l])
        res = _gmlp_call(grp, l, h, w_in, w_branch, gm_vnorm_g, wbd, bias, want_v)
        oa = res[0]
        if want_v:
            vs.append(res[1])
        ob, conv_n, h_n = _rglru_call(grp, l, h, w_in, w_branch, conv_w, conv_b, wa, ba, wi, bi, sp,
                                      conv0[l], h0[l])
        oc, s_n = _hgrn2_call(grp, l, h, w_in, w_branch, loglb, log1mlb, omlb, onorm_g, s0[l])
        convs.append(conv_n)
        hs.append(h_n[:, 0, :])
        ss.append(s_n)
        if l + 1 < DEPTH:
            x, h = _outproj_call(grp, l, oa, ob, oc, x, w_out, gate[l], norm_g, scale[l + 1], shift[l + 1])
        else:
            y = _outproj_call(grp, l, oa, ob, oc, x, w_out, gate[l], final_g)
    return y, jnp.stack(convs), jnp.stack(hs), jnp.stack(ss), (jnp.stack(vs) if want_v else None)


def kernel(x_prompt, x_sample, c_prompt, c_sample, state_rglru_conv, state_rglru_h, state_hgrn2,
           w_ada, b_ada, norm_g, w_in, gm_vnorm_g, gm_ws, gm_bs, lru_conv_w, lru_conv_b,
           lru_wa, lru_ba, lru_wx, lru_bx, lru_lambda, hg_lb, hg_onorm_g, w_branch, w_out, final_g):
    batch_p, seq_p, _ = x_prompt.shape
    batch_s, seq_s, _ = x_sample.shape
    grp_p, grp_s = _Group(batch_p, seq_p), _Group(batch_s, seq_s)

    p = jax.nn.softmax(hg_lb.astype(F32), axis=0)
    cs = jnp.cumsum(p, axis=0)
    lbs = cs - cs[0]
    row = lambda a: a.reshape(DEPTH, 1, a.shape[-1])
    weights = (row(norm_g), w_in.astype(BF16), row(gm_vnorm_g), gm_ws, gm_bs,
               lru_conv_w, row(lru_conv_b), lru_wa.astype(BF16), row(lru_ba), lru_wx.astype(BF16), row(lru_bx),
               row(jax.nn.softplus(-lru_lambda.astype(F32))),
               row(jnp.log(lbs)), row(jnp.log1p(-lbs)), row(1.0 - lbs), row(hg_onorm_g),
               w_branch.astype(BF16), w_out.astype(BF16), final_g.reshape(1, D_MODEL))

    mod = _mod_call(jnp.concatenate([c_prompt, c_sample], axis=0), w_ada, b_ada)
    dt = x_prompt.dtype
    y_p, conv_p, h_p, s_p, _ = _run_group(
        grp_p, x_prompt, mod[:, :batch_p],
        jnp.zeros((DEPTH, batch_p, LRU_CONV - 1, LRU_WIDTH), dt),
        jnp.zeros((DEPTH, batch_p, 1, LRU_WIDTH), dt),
        jnp.zeros((DEPTH, batch_p, HG_HEADS, HG_DK, HG_DV), dt), False, weights)
    y_s, conv_s, h_s, s_s, v_s = _run_group(
        grp_s, x_sample, mod[:, batch_p:],
        state_rglru_conv, state_rglru_h[:, :, None, :], state_hgrn2, True, weights)
    return (y_p, y_s, conv_p, h_p, s_p, conv_s, h_s, s_s, v_s)
```

```python
import functools

import jax
import jax.numpy as jnp
from jax import lax
from jax.experimental import pallas as pl
from jax.experimental.pallas import tpu as pltpu

F32 = jnp.float32
BF16 = jnp.bfloat16

D_MODEL = 2048
DEPTH = 4
GM_CHUNK = 128
GM_GROUPS = 8
GM_WIDTH = 1024
LRU_WIDTH = 1024
LRU_BLOCKS = 8
LRU_CONV = 4
LRU_C = 8.0
HG_CHUNK = 64
HG_HEADS = 8
HG_DK = 128
HG_DV = 128
HG_WIDTH = HG_HEADS * HG_DV
EPS = 1e-6

LANES = 128
SUBLANES = 8
COL = 1024
COL_GM_U, COL_GM_V, COL_GM_G = 0, 1, 2
COL_LRU_X, COL_LRU_G = 3, 4
COL_HG_Q, COL_HG_F, COL_HG_I, COL_HG_O = 5, 6, 7, 8
COL_MERGE = 9
TILE_ROWS = 256
HG_MULTISEQ_ROWS = 128
HG_SUB = 16
HG_CLAMP = 60.0
CONV_PAD = 8
V7X_VMEM_BYTES = 64 * 1024 * 1024


class _Group:
    def __init__(self, batch, seq, tile_rows=TILE_ROWS):
        self.batch, self.seq = batch, seq
        self.ts = min(seq, tile_rows)
        self.nb = tile_rows // self.ts
        assert batch % self.nb == 0 and seq % self.ts == 0
        self.tm = self.nb * self.ts
        self.grid = (batch // self.nb, seq // self.ts)
        self.gm_chunk = min(seq, GM_CHUNK)
        self.hg_chunk = min(seq, HG_CHUNK)
        assert self.ts % self.gm_chunk == 0 and self.ts % self.hg_chunk == 0
        assert self.hg_chunk % HG_SUB == 0
        assert self.nb == 1 or self.grid[1] == 1


def _dot(a, b):
    return jnp.dot(a, b, preferred_element_type=F32)


def _rmsnorm(x, g):
    return x * lax.rsqrt(jnp.mean(x * x, axis=-1, keepdims=True) + EPS) * g


def _adaln(x, g, scale, shift):
    return (_rmsnorm(x, g) * (1.0 + scale) + shift).astype(BF16)


def _tok_spec(grp, width):
    return pl.BlockSpec((grp.nb, grp.ts, width), lambda b, j: (b, j, 0))


def _seq_spec(grp, *tail):
    zeros = (0,) * len(tail)
    return pl.BlockSpec((grp.nb,) + tail, lambda b, j: (b,) + zeros)


def _const_spec(shape):
    nd = len(shape)
    return pl.BlockSpec(shape, lambda b, j: (0,) * nd, pipeline_mode=pl.Buffered(1))


def _layer_spec(layer, *tail):
    zeros = (0,) * len(tail)
    return pl.BlockSpec((None,) + tail, lambda b, j: (layer,) + zeros, pipeline_mode=pl.Buffered(1))


def _win_spec(layer, col):
    return pl.BlockSpec((None, D_MODEL, COL), lambda b, j: (layer, 0, col), pipeline_mode=pl.Buffered(1))


def _wbr_spec(layer, branch):
    return pl.BlockSpec((None, COL, D_MODEL), lambda b, j: (layer, branch, 0), pipeline_mode=pl.Buffered(1))


def _params(vmem_bytes):
    return pltpu.CompilerParams(dimension_semantics=("arbitrary", "arbitrary"),
                                vmem_limit_bytes=min(int(vmem_bytes), V7X_VMEM_BYTES - (4 << 20)))


def _merge_gates(hh, mg0_ref, mg1_ref):
    return [jax.nn.sigmoid(_dot(hh, mg_ref[...])) for mg_ref in (mg0_ref, mg1_ref)]


def _gated_out(gates, y, wbr_ref):
    return jnp.concatenate(
        [g * _dot(y, wbr_ref[:, half * COL:(half + 1) * COL]) for half, g in enumerate(gates)], axis=-1)


def _scan_rows(a, b, seg):
    rows, width = b.shape
    groups = rows // SUBLANES
    b3 = b.reshape(groups, SUBLANES, width)
    a3 = None if a is None else a.reshape(groups, SUBLANES, width)
    pos = lax.broadcasted_iota(jnp.int32, (1, SUBLANES, 1), 1)
    d = 1
    while d < SUBLANES:
        keep = pos >= d
        b_prev = pltpu.roll(b3, d, 1)
        if a3 is None:
            b3 = jnp.where(keep, b3 + b_prev, b3)
        else:
            b3 = jnp.where(keep, a3 * b_prev + b3, b3)
            a3 = jnp.where(keep, a3 * pltpu.roll(a3, d, 1), a3)
        d *= 2
    per_seg = seg // SUBLANES
    out = []
    for g in range(groups):
        hg = b3[g]
        if g % per_seg:
            carry = out[-1][SUBLANES - 1:SUBLANES, :]
            hg = hg + carry if a3 is None else a3[g] * carry + hg
        out.append(hg)
    return jnp.concatenate(out, axis=0)


def _mod_kernel(c_ref, w_ref, b_ref, o_ref):
    sc = jax.nn.silu(c_ref[...]).astype(BF16)
    o_ref[...] = _dot(sc, w_ref[...].astype(BF16)) + b_ref[...]


def _mod_call(c_all, w_ada, b_ada):
    rows = c_all.shape[0]
    width = 3 * D_MODEL
    tn = 1024
    return pl.pallas_call(
        _mod_kernel,
        grid=(DEPTH, width // tn),
        in_specs=[pl.BlockSpec((rows, D_MODEL), lambda l, n: (0, 0)),
                  pl.BlockSpec((None, D_MODEL, tn), lambda l, n: (l, 0, n)),
                  pl.BlockSpec((None, 1, tn), lambda l, n: (l, 0, n))],
        out_specs=pl.BlockSpec((None, rows, tn), lambda l, n: (l, 0, n)),
        out_shape=jax.ShapeDtypeStruct((DEPTH, rows, width), F32),
        compiler_params=_params(2 * D_MODEL * tn * 4 + D_MODEL * tn * 2 + (8 << 20)),
        name="adaln_mod",
    )(c_all, w_ada, b_ada.reshape(DEPTH, 1, width))


def _prenorm_kernel(x_ref, g_ref, scale_ref, shift_ref, h_ref):
    h_ref[...] = _adaln(x_ref[...], g_ref[...], scale_ref[...], shift_ref[...])


def _prenorm_call(grp, layer, x, norm_g, scale, shift):
    return pl.pallas_call(
        _prenorm_kernel,
        grid=grp.grid,
        in_specs=[_tok_spec(grp, D_MODEL), _layer_spec(layer, 1, D_MODEL),
                  _seq_spec(grp, 1, D_MODEL), _seq_spec(grp, 1, D_MODEL)],
        out_specs=_tok_spec(grp, D_MODEL),
        out_shape=jax.ShapeDtypeStruct(x.shape, BF16),
        compiler_params=_params(grp.tm * D_MODEL * (2 * 4 + 2 * 2 + 3 * 4) + (4 << 20)),
        name="prenorm",
    )(x, norm_g, scale, shift)


def _gmlp_kernel(h_ref, wu_ref, wv_ref, wg_ref, mg0_ref, mg1_ref, wbr_ref, vg_ref, wbd_ref, bias_ref,
                 out_ref, *v_out, grp):
    hh = h_ref[...].reshape(grp.tm, D_MODEL)
    v = _rmsnorm(jax.nn.gelu(_dot(hh, wv_ref[...])), vg_ref[...])
    if v_out:
        v_out[0][...] = v.reshape(grp.nb, grp.ts, GM_WIDTH)
    vb = v.astype(BF16)
    width = GM_WIDTH // GM_GROUPS
    s = jnp.concatenate([_dot(wbd_ref[g], vb[:, g * width:(g + 1) * width]) for g in range(GM_GROUPS)],
                        axis=-1) + bias_ref[...]
    u = jax.nn.gelu(_dot(hh, wu_ref[...]))
    ya = (u * s) * jax.nn.silu(_dot(hh, wg_ref[...]))
    oa = _gated_out(_merge_gates(hh, mg0_ref, mg1_ref), ya.astype(BF16), wbr_ref)
    out_ref[...] = oa.astype(BF16).reshape(grp.nb, grp.ts, D_MODEL)


def _gmlp_call(grp, layer, h, w_in, w_branch, vnorm_g, wbd, bias, want_v):
    tm = grp.tm
    out_shape = [jax.ShapeDtypeStruct((grp.batch, grp.seq, D_MODEL), BF16)]
    out_specs = [_tok_spec(grp, D_MODEL)]
    if want_v:
        out_shape.append(jax.ShapeDtypeStruct((grp.batch, grp.seq, GM_WIDTH), F32))
        out_specs.append(_tok_spec(grp, GM_WIDTH))
    resident = (5 * D_MODEL * COL + COL * D_MODEL + GM_GROUPS * tm * tm) * 2 + tm * GM_WIDTH * 4
    streamed = 2 * tm * (D_MODEL * 2 + D_MODEL * 2 + GM_WIDTH * 4)
    temps = 12 * tm * GM_WIDTH * 4
    return pl.pallas_call(
        functools.partial(_gmlp_kernel, grp=grp),
        grid=grp.grid,
        in_specs=[_tok_spec(grp, D_MODEL),
                  _win_spec(layer, COL_GM_U), _win_spec(layer, COL_GM_V), _win_spec(layer, COL_GM_G),
                  _win_spec(layer, COL_MERGE), _win_spec(layer, COL_MERGE + 1), _wbr_spec(layer, 0),
                  _layer_spec(layer, 1, GM_WIDTH), _const_spec(wbd.shape), _const_spec(bias.shape)],
        out_specs=out_specs,
        out_shape=out_shape,
        compiler_params=_params(resident + streamed + temps),
        name="gmlp_branch",
    )(h, w_in, w_in, w_in, w_in, w_in, w_branch, vnorm_g, wbd, bias)


def _hgrn2_kernel(h_ref, wq_ref, wf_ref, wi_ref, wo_ref, mg0_ref, mg1_ref, wbr_ref,
                  loglb_ref, log1mlb_ref, omlb_ref, og_ref, s0_ref,
                  out_ref, sn_ref,
                  qe_scr, qt_scr, kd_scr, kt_scr, v_scr, inter_scr, intra_scr, cq_scr, ck_scr, cv_scr, cg_scr,
                  *, grp):
    nb, tm, L = grp.nb, grp.tm, grp.hg_chunk
    n_chunks = tm // L
    n_sub = L // HG_SUB
    W = HG_WIDTH

    @pl.when(pl.program_id(1) == 0)
    def _():
        sn_ref[...] = s0_ref[...]

    hh = h_ref[...].reshape(tm, D_MODEL)

    def gates(rows_h):
        q = jax.nn.silu(_dot(rows_h, wq_ref[...]))
        zf = _dot(rows_h, wf_ref[...])
        log_f = jnp.logaddexp(loglb_ref[...], log1mlb_ref[...] + jax.nn.log_sigmoid(zf))
        k = omlb_ref[...] * jax.nn.sigmoid(-zf)
        return q, k, log_f, _dot(rows_h, wi_ref[...])

    q, k, log_f, v = gates(hh)
    shape4 = (n_chunks, n_sub, HG_SUB, W)
    g_sub = _scan_rows(None, log_f, HG_SUB).reshape(shape4)
    totals = g_sub[:, :, HG_SUB - 1:HG_SUB, :]
    refs = []
    acc = jnp.zeros((n_chunks, 1, 1, W), F32)
    for i in range(n_sub):
        refs.append(acc)
        acc = acc + totals[:, i:i + 1]
    g_last = acc
    G = g_sub + jnp.concatenate(refs, axis=1)
    q4, k4 = q.reshape(shape4), k.reshape(shape4)
    operand = lambda a: a.reshape(tm, W).astype(BF16)
    qe_scr[...] = operand(q4 * jnp.exp(G))
    qt_scr[...] = operand(q4 * jnp.exp(g_sub))
    kd_scr[...] = operand(k4 * jnp.exp(g_last - G))
    for i in range(n_sub):
        kt_scr[i] = operand(k4 * jnp.exp(jnp.minimum(refs[i] - G, HG_CLAMP)))
    v_scr[...] = v.astype(BF16)
    chunk_decay = jnp.exp(g_last).reshape(n_chunks, W)
    out_gate = jax.nn.silu(_dot(hh, wo_ref[...]))
    merge_gates = _merge_gates(hh, mg0_ref, mg1_ref)

    tril = lax.broadcasted_iota(jnp.int32, (L, L), 0) >= lax.broadcasted_iota(jnp.int32, (L, L), 1)
    heads = range(HG_HEADS)
    head_cols = [slice(hd * HG_DK, (hd + 1) * HG_DK) for hd in heads]
    chunk_rows = [slice(c * L, (c + 1) * L) for c in range(n_chunks)]
    contract_last = (((1,), (1,)), ((), ()))
    contract_rows = (((0,), (0,)), ((), ()))
    for c, rows in enumerate(chunk_rows):
        scores = []
        for cols in head_cols:
            a_rows = [lax.dot_general(qt_scr[c * L + i * HG_SUB:c * L + (i + 1) * HG_SUB, cols], kt_scr[i, rows, cols],
                                      contract_last, preferred_element_type=F32) for i in range(n_sub)]
            scores.append(jnp.concatenate(a_rows, axis=0))
        scores = [jnp.where(tril, A, 0.0).astype(BF16) for A in scores]
        intra_scr[rows, :] = jnp.concatenate([_dot(A, v_scr[rows, cols]) for A, cols in zip(scores, head_cols)], axis=-1)
    S = None
    for c, rows in enumerate(chunk_rows):
        slot = 0 if nb == 1 else c
        if nb > 1 or c == 0:
            S = [sn_ref[slot, hd] for hd in heads]
        inter_scr[rows, :] = jnp.concatenate(
            [_dot(qe_scr[rows, cols], S[hd].astype(BF16)) for hd, cols in enumerate(head_cols)], axis=-1)
        update = [lax.dot_general(kd_scr[rows, cols], v_scr[rows, cols], contract_rows, preferred_element_type=F32)
                  for cols in head_cols]
        S = [jnp.transpose(jnp.broadcast_to(chunk_decay[c:c + 1, cols], (HG_DV, HG_DK))) * S[hd] + update[hd]
             for hd, cols in enumerate(head_cols)]
        if nb > 1 or c == n_chunks - 1:
            for hd in heads:
                sn_ref[slot, hd] = S[hd]

    @pl.when(jnp.min(totals) < -HG_CLAMP)
    def _():
        t_idx = lax.broadcasted_iota(jnp.int32, (L, 1), 0)
        for c in range(n_chunks):
            rows = slice(c * L, (c + 1) * L)
            cq_scr[...], ck_scr[...], log_f_c, cv_scr[...] = gates(hh[rows])
            cg_scr[...] = _scan_rows(None, log_f_c, L)
            for hd in range(HG_HEADS):
                cols = slice(hd * HG_DK, (hd + 1) * HG_DK)
                G_c, q_c = cg_scr[:, cols], cq_scr[:, cols]

                def add_sources(blk, acc):
                    src = pl.ds(pl.multiple_of(blk * SUBLANES, SUBLANES), SUBLANES)
                    g_blk, k_blk, v_blk = cg_scr[src, cols], ck_scr[src, cols], cv_scr[src, cols]
                    for r in range(SUBLANES):
                        g_s, k_s, v_s = g_blk[r:r + 1, :], k_blk[r:r + 1, :], v_blk[r:r + 1, :]
                        w = jnp.sum(q_c * k_s * jnp.exp(jnp.minimum(G_c - g_s, 0.0)), axis=-1, keepdims=True)
                        acc = acc + jnp.where(t_idx >= blk * SUBLANES + r, w, 0.0) * v_s
                    return acc

                intra_scr[rows, cols] = lax.fori_loop(0, L // SUBLANES, add_sources, jnp.zeros((L, HG_DV), F32))

    o = inter_scr[...] + intra_scr[...]
    y = jnp.concatenate(
        [_rmsnorm(o[:, hd * HG_DV:(hd + 1) * HG_DV], og_ref[...]) for hd in range(HG_HEADS)], axis=-1)
    oc = _gated_out(merge_gates, (y * out_gate).astype(BF16), wbr_ref)
    out_ref[...] = oc.astype(BF16).reshape(grp.nb, grp.ts, D_MODEL)


def _hgrn2_call(grp, layer, h, w_in, w_branch, loglb, log1mlb, omlb, onorm_g, s0):
    tm, L = grp.tm, grp.hg_chunk
    n_sub = L // HG_SUB
    state = grp.nb * HG_HEADS * HG_DK * HG_DV * 4
    resident = (6 * D_MODEL * COL + COL * D_MODEL) * 2
    streamed = 2 * tm * (D_MODEL * 2 + D_MODEL * 2) + 3 * state
    scratch = (4 + n_sub) * tm * HG_WIDTH * 2 + 2 * tm * HG_WIDTH * 4 + 4 * L * HG_WIDTH * 4
    temps = 10 * tm * HG_WIDTH * 4
    tile16 = pltpu.VMEM((tm, HG_WIDTH), BF16)
    tile32 = pltpu.VMEM((tm, HG_WIDTH), F32)
    chunk32 = pltpu.VMEM((L, HG_WIDTH), F32)
    state_in = pl.BlockSpec((grp.nb, HG_HEADS, HG_DK, HG_DV), lambda b, j: (b, 0, 0, 0),
                            pipeline_mode=pl.Buffered(1))
    return pl.pallas_call(
        functools.partial(_hgrn2_kernel, grp=grp),
        grid=grp.grid,
        in_specs=[_tok_spec(grp, D_MODEL),
                  _win_spec(layer, COL_HG_Q), _win_spec(layer, COL_HG_F), _win_spec(layer, COL_HG_I),
                  _win_spec(layer, COL_HG_O),
                  _win_spec(layer, COL_MERGE + 4), _win_spec(layer, COL_MERGE + 5), _wbr_spec(layer, 2),
                  _layer_spec(layer, 1, HG_WIDTH), _layer_spec(layer, 1, HG_WIDTH), _layer_spec(layer, 1, HG_WIDTH),
                  _layer_spec(layer, 1, HG_DV), state_in],
        out_specs=[_tok_spec(grp, D_MODEL), _seq_spec(grp, HG_HEADS, HG_DK, HG_DV)],
        out_shape=[jax.ShapeDtypeStruct((grp.batch, grp.seq, D_MODEL), BF16),
                   jax.ShapeDtypeStruct((grp.batch, HG_HEADS, HG_DK, HG_DV), F32)],
        scratch_shapes=[tile16, tile16, tile16, pltpu.VMEM((n_sub, tm, HG_WIDTH), BF16), tile16,
                        tile32, tile32, chunk32, chunk32, chunk32, chunk32],
        compiler_params=_params(resident + streamed + scratch + temps),
        name="hgrn2_branch",
    )(h, w_in, w_in, w_in, w_in, w_in, w_in, w_branch, loglb, log1mlb, omlb, onorm_g, s0)


def _rglru_kernel(h_ref, wx_ref, wg_ref, mg0_ref, mg1_ref, wbr_ref, convw_ref, convb_ref, wa_ref, ba_ref,
                  wi_ref, bi_ref, sp_ref, conv0_ref, h0_ref, oa_ref, oc_ref, x_ref, wout_ref, gate_ref, g_ref,
                  *rest, grp, last):
    if last:
        convn_ref, hn_ref, y_ref, xbuf, hcar = rest
    else:
        scale_ref, shift_ref, convn_ref, hn_ref, xn_ref, hnext_ref, xbuf, hcar = rest
    nb, ts, tm = grp.nb, grp.ts, grp.tm
    hist = CONV_PAD - (LRU_CONV - 1)

    @pl.when(pl.program_id(1) == 0)
    def _():
        xbuf[:, hist:CONV_PAD, :] = conv0_ref[...]
        hcar[...] = h0_ref[...]

    hh = h_ref[...].reshape(tm, D_MODEL)
    xbuf[:, CONV_PAD:CONV_PAD + ts, :] = _dot(hh, wx_ref[...]).reshape(nb, ts, LRU_WIDTH)
    y = convb_ref[...]
    for tap in range(LRU_CONV):
        y = y + convw_ref[tap:tap + 1, :] * xbuf[:, hist + tap:hist + tap + ts, :]
    conv_new = xbuf[:, CONV_PAD + ts - (LRU_CONV - 1):CONV_PAD + ts, :]
    xbuf[:, hist:CONV_PAD, :] = conv_new
    convn_ref[...] = conv_new

    x = y.reshape(tm, LRU_WIDTH)
    xb = x.astype(BF16)
    width = LRU_WIDTH // LRU_BLOCKS

    def block_diag(w_ref, b_ref):
        return jnp.concatenate([_dot(xb[:, n * width:(n + 1) * width], w_ref[n]) for n in range(LRU_BLOCKS)],
                               axis=-1) + b_ref[...]

    gate_r = jax.nn.sigmoid(block_diag(wa_ref, ba_ref))
    gate_i = jax.nn.sigmoid(block_diag(wi_ref, bi_ref))
    log_a = -LRU_C * gate_r * sp_ref[...]
    a = jnp.exp(log_a)
    drive = jnp.sqrt(-jnp.tanh(log_a) * (a * a + 1.0)) * gate_i * x
    h0 = jnp.broadcast_to(hcar[...], (nb, ts, LRU_WIDTH)).reshape(tm, LRU_WIDTH)
    first = lax.broadcasted_iota(jnp.int32, (tm, 1), 0) % ts == 0
    drive = drive + jnp.where(first, a * h0, 0.0)
    hseq = _scan_rows(a, drive, ts)
    h_last = hseq.reshape(nb, ts, LRU_WIDTH)[:, ts - 1:ts, :]
    hcar[...] = h_last
    hn_ref[...] = h_last
    yb = hseq * jax.nn.silu(_dot(hh, wg_ref[...]))
    ob = _gated_out(_merge_gates(hh, mg0_ref, mg1_ref), yb.astype(BF16), wbr_ref)

    as_rows = lambda ref: ref[...].reshape(tm, D_MODEL).astype(F32)
    merged = (as_rows(oa_ref) + ob) + as_rows(oc_ref)
    out = _dot(merged.astype(BF16), wout_ref[...]).reshape(nb, ts, D_MODEL)
    xn = x_ref[...] + gate_ref[...] * out
    if last:
        y_ref[...] = _rmsnorm(xn, g_ref[...])
    else:
        xn_ref[...] = xn
        hnext_ref[...] = _adaln(xn, g_ref[...], scale_ref[...], shift_ref[...])


def _rglru_call(grp, layer, h, w_in, w_branch, w_out, conv_w, conv_b, wa, ba, wi, bi, sp, conv0, h0,
                oa, oc, x, gate, norm_g, scale=None, shift=None):
    last = scale is None
    tm = grp.tm
    tok = _tok_spec(grp, D_MODEL)
    in_specs = [tok,
                _win_spec(layer, COL_LRU_X), _win_spec(layer, COL_LRU_G),
                _win_spec(layer, COL_MERGE + 2), _win_spec(layer, COL_MERGE + 3), _wbr_spec(layer, 1),
                _layer_spec(layer, LRU_CONV, LRU_WIDTH), _layer_spec(layer, 1, LRU_WIDTH),
                _layer_spec(layer, LRU_BLOCKS, LANES, LANES), _layer_spec(layer, 1, LRU_WIDTH),
                _layer_spec(layer, LRU_BLOCKS, LANES, LANES), _layer_spec(layer, 1, LRU_WIDTH),
                _layer_spec(layer, 1, LRU_WIDTH),
                _seq_spec(grp, LRU_CONV - 1, LRU_WIDTH), _seq_spec(grp, 1, LRU_WIDTH),
                tok, tok, tok, _layer_spec(layer, D_MODEL, D_MODEL), _seq_spec(grp, 1, D_MODEL)]
    args = [h, w_in, w_in, w_in, w_in, w_branch, conv_w, conv_b, wa, ba, wi, bi, sp, conv0, h0,
            oa, oc, x, w_out, gate]
    out_specs = [_seq_spec(grp, LRU_CONV - 1, LRU_WIDTH), _seq_spec(grp, 1, LRU_WIDTH)]
    out_shape = [jax.ShapeDtypeStruct((grp.batch, LRU_CONV - 1, LRU_WIDTH), F32),
                 jax.ShapeDtypeStruct((grp.batch, 1, LRU_WIDTH), F32)]
    if last:
        in_specs.append(pl.BlockSpec((1, D_MODEL), lambda b, j: (0, 0)))
        args.append(norm_g)
        out_specs.append(tok)
        out_shape.append(jax.ShapeDtypeStruct(x.shape, F32))
    else:
        in_specs += [_layer_spec(layer + 1, 1, D_MODEL), _seq_spec(grp, 1, D_MODEL), _seq_spec(grp, 1, D_MODEL)]
        args += [norm_g, scale, shift]
        out_specs += [tok, tok]
        out_shape += [jax.ShapeDtypeStruct(x.shape, F32), jax.ShapeDtypeStruct(x.shape, BF16)]
    resident = (4 * D_MODEL * COL + COL * D_MODEL + D_MODEL * D_MODEL + 2 * LRU_WIDTH * LRU_WIDTH // LRU_BLOCKS) * 2
    streamed = 2 * tm * D_MODEL * (3 * 2 + 2 * 4 + 2)
    scratch = grp.nb * (CONV_PAD + grp.ts) * LRU_WIDTH * 4
    temps = 8 * tm * D_MODEL * 4
    return pl.pallas_call(
        functools.partial(_rglru_kernel, grp=grp, last=last),
        grid=grp.grid,
        in_specs=in_specs,
        out_specs=out_specs,
        out_shape=out_shape,
        scratch_shapes=[pltpu.VMEM((grp.nb, CONV_PAD + grp.ts, LRU_WIDTH), F32),
                        pltpu.VMEM((grp.nb, 1, LRU_WIDTH), F32)],
        compiler_params=_params(resident + streamed + scratch + temps),
        name="rglru_merge",
    )(*args)


def _gmlp_mix_weights(grp, ws, bs):
    L = grp.gm_chunk
    reps = grp.tm // L
    w = jnp.where(jnp.tril(jnp.ones((L, L), dtype=bool))[None], ws[:, :L, :L], 0.0)
    wbd = jnp.einsum('ab,gts->gatbs', jnp.eye(reps, dtype=F32), w).reshape(GM_GROUPS, grp.tm, grp.tm)
    bias = jnp.repeat(jnp.tile(bs[:, :L].T, (reps, 1)), GM_WIDTH // GM_GROUPS, axis=1)
    return wbd.astype(BF16), bias


def _run_group(grp, x, mod, conv0, h0, s0, want_v, weights):
    (norm_g, w_in, gm_vnorm_g, gm_ws, gm_bs, conv_w, conv_b, wa, ba, wi, bi, sp,
     loglb, log1mlb, omlb, onorm_g, w_branch, w_out, final_g) = weights
    shift = mod[:, :, None, 0:D_MODEL]
    scale = mod[:, :, None, D_MODEL:2 * D_MODEL]
    gate = mod[:, :, None, 2 * D_MODEL:]
    h = _prenorm_call(grp, 0, x, norm_g, scale[0], shift[0])
    hg_grp = grp if grp.nb == 1 else _Group(grp.batch, grp.seq, HG_MULTISEQ_ROWS)
    convs, hs, ss, vs = [], [], [], []
    y = None
    for l in range(DEPTH):
        wbd, bias = _gmlp_mix_weights(grp, gm_ws[l], gm_bs[l])
        res = _gmlp_call(grp, l, h, w_in, w_branch, gm_vnorm_g, wbd, bias, want_v)
        oa = res[0]
        if want_v:
            vs.append(res[1])
        oc, s_n = _hgrn2_call(hg_grp, l, h, w_in, w_branch, loglb, log1mlb, omlb, onorm_g, s0[l])
        lru = (grp, l, h, w_in, w_branch, w_out, conv_w, conv_b, wa, ba, wi, bi, sp, conv0[l], h0[l], oa, oc, x, gate[l])
        if l + 1 < DEPTH:
            conv_n, h_n, x, h = _rglru_call(*lru, norm_g, scale[l + 1], shift[l + 1])
        else:
            conv_n, h_n, y = _rglru_call(*lru, final_g)
        convs.append(conv_n)
        hs.append(h_n[:, 0, :])
        ss.append(s_n)
    return y, jnp.stack(convs), jnp.stack(hs), jnp.stack(ss), (jnp.stack(vs) if want_v else None)


def kernel(x_prompt, x_sample, c_prompt, c_sample, state_rglru_conv, state_rglru_h, state_hgrn2,
           w_ada, b_ada, norm_g, w_in, gm_vnorm_g, gm_ws, gm_bs, lru_conv_w, lru_conv_b,
           lru_wa, lru_ba, lru_wx, lru_bx, lru_lambda, hg_lb, hg_onorm_g, w_branch, w_out, final_g):
    batch_p, seq_p, _ = x_prompt.shape
    batch_s, seq_s, _ = x_sample.shape
    grp_p, grp_s = _Group(batch_p, seq_p), _Group(batch_s, seq_s)

    p = jax.nn.softmax(hg_lb.astype(F32), axis=0)
    cs = jnp.cumsum(p, axis=0)
    lbs = cs - cs[0]
    row = lambda a: a.reshape(DEPTH, 1, a.shape[-1])
    weights = (row(norm_g), w_in.astype(BF16), row(gm_vnorm_g), gm_ws, gm_bs,
               lru_conv_w, row(lru_conv_b), lru_wa.astype(BF16), row(lru_ba), lru_wx.astype(BF16), row(lru_bx),
               row(jax.nn.softplus(-lru_lambda.astype(F32))),
               row(jnp.log(lbs)), row(jnp.log1p(-lbs)), row(1.0 - lbs), row(hg_onorm_g),
               w_branch.astype(BF16), w_out.astype(BF16), final_g.reshape(1, D_MODEL))

    mod = _mod_call(jnp.concatenate([c_prompt, c_sample], axis=0), w_ada, b_ada)
    dt = x_prompt.dtype
    y_p, conv_p, h_p, s_p, _ = _run_group(
        grp_p, x_prompt, mod[:, :batch_p],
        jnp.zeros((DEPTH, batch_p, LRU_CONV - 1, LRU_WIDTH), dt),
        jnp.zeros((DEPTH, batch_p, 1, LRU_WIDTH), dt),
        jnp.zeros((DEPTH, batch_p, HG_HEADS, HG_DK, HG_DV), dt), False, weights)
    y_s, conv_s, h_s, s_s, v_s = _run_group(
        grp_s, x_sample, mod[:, batch_p:],
        state_rglru_conv, state_rglru_h[:, :, None, :], state_hgrn2, True, weights)
    return (y_p, y_s, conv_p, h_p, s_p, conv_s, h_s, s_s, v_s)
```

```python
import functools

import jax
import jax.numpy as jnp
from jax import lax
from jax.experimental import pallas as pl
from jax.experimental.pallas import tpu as pltpu

F32 = jnp.float32
BF16 = jnp.bfloat16

D_MODEL = 2048
DEPTH = 4
GM_CHUNK = 128
GM_GROUPS = 8
GM_WIDTH = 1024
LRU_WIDTH = 1024
LRU_BLOCKS = 8
LRU_CONV = 4
LRU_C = 8.0
HG_CHUNK = 64
HG_HEADS = 8
HG_DK = 128
HG_DV = 128
HG_WIDTH = HG_HEADS * HG_DV
EPS = 1e-6

LANES = 128
SUBLANES = 8
COL = 1024
COL_GM_U, COL_GM_V, COL_GM_G = 0, 1, 2
COL_LRU_X, COL_LRU_G = 3, 4
COL_HG_Q, COL_HG_F, COL_HG_I, COL_HG_O = 5, 6, 7, 8
COL_MERGE = 9
TILE_ROWS = 256
GM_TILE_ROWS = 512
NORM_TILE_ROWS = 1024
HG_MULTISEQ_ROWS = 128
HG_SUB = 16
HG_CLAMP = 60.0
CONV_PAD = 8
V7X_VMEM_BYTES = 64 * 1024 * 1024


class _Group:
    def __init__(self, batch, seq, tile_rows=TILE_ROWS):
        self.batch, self.seq = batch, seq
        self.ts = min(seq, tile_rows)
        self.nb = min(tile_rows // self.ts, batch)
        assert batch % self.nb == 0 and seq % self.ts == 0
        self.tm = self.nb * self.ts
        self.grid = (batch // self.nb, seq // self.ts)
        self.gm_chunk = min(seq, GM_CHUNK)
        self.hg_chunk = min(seq, HG_CHUNK)
        assert self.ts % self.gm_chunk == 0 and self.ts % self.hg_chunk == 0
        assert self.hg_chunk % HG_SUB == 0
        assert self.nb == 1 or self.grid[1] == 1


def _dot(a, b):
    return jnp.dot(a, b, preferred_element_type=F32)


def _rmsnorm(x, g):
    return x * lax.rsqrt(jnp.mean(x * x, axis=-1, keepdims=True) + EPS) * g


def _adaln(x, g, scale, shift):
    return (_rmsnorm(x, g) * (1.0 + scale) + shift).astype(BF16)


def _tok_spec(grp, width):
    return pl.BlockSpec((grp.nb, grp.ts, width), lambda b, j: (b, j, 0))


def _seq_spec(grp, *tail):
    zeros = (0,) * len(tail)
    return pl.BlockSpec((grp.nb,) + tail, lambda b, j: (b,) + zeros)


def _layer_seq_spec(grp, layer, *tail, **kwargs):
    zeros = (0,) * len(tail)
    return pl.BlockSpec((None, grp.nb) + tail, lambda b, j: (layer, b) + zeros, **kwargs)


def _const_spec(shape):
    nd = len(shape)
    return pl.BlockSpec(shape, lambda b, j: (0,) * nd, pipeline_mode=pl.Buffered(1))


def _layer_spec(layer, *tail):
    zeros = (0,) * len(tail)
    return pl.BlockSpec((None,) + tail, lambda b, j: (layer,) + zeros, pipeline_mode=pl.Buffered(1))


def _win_spec(layer, col):
    return pl.BlockSpec((None, D_MODEL, COL), lambda b, j: (layer, 0, col), pipeline_mode=pl.Buffered(1))


def _wbr_spec(layer, branch):
    return pl.BlockSpec((None, COL, D_MODEL), lambda b, j: (layer, branch, 0), pipeline_mode=pl.Buffered(1))


def _params(vmem_bytes):
    return pltpu.CompilerParams(dimension_semantics=("arbitrary", "arbitrary"),
                                vmem_limit_bytes=min(int(vmem_bytes), V7X_VMEM_BYTES - (4 << 20)))


def _merge_gates(hh, mg0_ref, mg1_ref):
    return [jax.nn.sigmoid(_dot(hh, mg_ref[...])) for mg_ref in (mg0_ref, mg1_ref)]


def _gated_out(gates, y, wbr_ref):
    return jnp.concatenate(
        [g * _dot(y, wbr_ref[:, half * COL:(half + 1) * COL]) for half, g in enumerate(gates)], axis=-1)


def _scan_rows(a, b, seg):
    rows, width = b.shape
    groups = rows // SUBLANES
    b3 = b.reshape(groups, SUBLANES, width)
    a3 = None if a is None else a.reshape(groups, SUBLANES, width)
    pos = lax.broadcasted_iota(jnp.int32, (1, SUBLANES, 1), 1)
    d = 1
    while d < SUBLANES:
        keep = pos >= d
        b_prev = pltpu.roll(b3, d, 1)
        if a3 is None:
            b3 = jnp.where(keep, b3 + b_prev, b3)
        else:
            b3 = jnp.where(keep, a3 * b_prev + b3, b3)
            a3 = jnp.where(keep, a3 * pltpu.roll(a3, d, 1), a3)
        d *= 2
    per_seg = seg // SUBLANES
    out = []
    for g in range(groups):
        hg = b3[g]
        if g % per_seg:
            carry = out[-1][SUBLANES - 1:SUBLANES, :]
            hg = hg + carry if a3 is None else a3[g] * carry + hg
        out.append(hg)
    return jnp.concatenate(out, axis=0)


def _mod_kernel(c_ref, w_ref, b_ref, o_ref):
    sc = jax.nn.silu(c_ref[...]).astype(BF16)
    o_ref[...] = _dot(sc, w_ref[...].astype(BF16)) + b_ref[...]


def _mod_call(c_all, w_ada, b_ada):
    rows = c_all.shape[0]
    width = 3 * D_MODEL
    tn = 1024
    return pl.pallas_call(
        _mod_kernel,
        grid=(DEPTH, width // tn),
        in_specs=[pl.BlockSpec((rows, D_MODEL), lambda l, n: (0, 0)),
                  pl.BlockSpec((None, D_MODEL, tn), lambda l, n: (l, 0, n)),
                  pl.BlockSpec((None, 1, tn), lambda l, n: (l, 0, n))],
        out_specs=pl.BlockSpec((None, rows, tn), lambda l, n: (l, 0, n)),
        out_shape=jax.ShapeDtypeStruct((DEPTH, rows, width), F32),
        compiler_params=_params(2 * D_MODEL * tn * 4 + D_MODEL * tn * 2 + (8 << 20)),
        name="adaln_mod",
    )(c_all, w_ada, b_ada.reshape(DEPTH, 1, width))


def _prenorm_kernel(x_ref, g_ref, scale_ref, shift_ref, h_ref):
    h_ref[...] = _adaln(x_ref[...], g_ref[...], scale_ref[...], shift_ref[...])


def _prenorm_call(grp, layer, x, norm_g, scale, shift):
    return pl.pallas_call(
        _prenorm_kernel,
        grid=grp.grid,
        in_specs=[_tok_spec(grp, D_MODEL), _layer_spec(layer, 1, D_MODEL),
                  _seq_spec(grp, 1, D_MODEL), _seq_spec(grp, 1, D_MODEL)],
        out_specs=_tok_spec(grp, D_MODEL),
        out_shape=jax.ShapeDtypeStruct(x.shape, BF16),
        compiler_params=_params(grp.tm * D_MODEL * (2 * 4 + 2 * 2 + 3 * 4) + (4 << 20)),
        name="prenorm",
    )(x, norm_g, scale, shift)


def _gmlp_kernel(h_ref, wu_ref, wv_ref, wg_ref, mg0_ref, mg1_ref, wbr_ref, vg_ref, wbd_ref, bias_ref,
                 out_ref, *v_out, grp):
    hh = h_ref[...].reshape(grp.tm, D_MODEL)
    v = _rmsnorm(jax.nn.gelu(_dot(hh, wv_ref[...])), vg_ref[...])
    if v_out:
        v_out[0][...] = v.reshape(grp.nb, grp.ts, GM_WIDTH)
    vb = v.astype(BF16)
    width = GM_WIDTH // GM_GROUPS
    s = jnp.concatenate([_dot(wbd_ref[g], vb[:, g * width:(g + 1) * width]) for g in range(GM_GROUPS)],
                        axis=-1) + bias_ref[...]
    u = jax.nn.gelu(_dot(hh, wu_ref[...]))
    ya = (u * s) * jax.nn.silu(_dot(hh, wg_ref[...]))
    oa = _gated_out(_merge_gates(hh, mg0_ref, mg1_ref), ya.astype(BF16), wbr_ref)
    out_ref[...] = oa.astype(BF16).reshape(grp.nb, grp.ts, D_MODEL)


def _gmlp_call(grp, layer, h, w_in, w_branch, vnorm_g, wbd, bias, want_v):
    tm = grp.tm
    out_shape = [jax.ShapeDtypeStruct((grp.batch, grp.seq, D_MODEL), BF16)]
    out_specs = [_tok_spec(grp, D_MODEL)]
    if want_v:
        out_shape.append(jax.ShapeDtypeStruct((grp.batch, grp.seq, GM_WIDTH), F32))
        out_specs.append(_tok_spec(grp, GM_WIDTH))
    resident = (5 * D_MODEL * COL + COL * D_MODEL + GM_GROUPS * tm * tm) * 2 + tm * GM_WIDTH * 4
    streamed = 2 * tm * (D_MODEL * 2 + D_MODEL * 2 + GM_WIDTH * 4)
    temps = 12 * tm * GM_WIDTH * 4
    return pl.pallas_call(
        functools.partial(_gmlp_kernel, grp=grp),
        grid=grp.grid,
        in_specs=[_tok_spec(grp, D_MODEL),
                  _win_spec(layer, COL_GM_U), _win_spec(layer, COL_GM_V), _win_spec(layer, COL_GM_G),
                  _win_spec(layer, COL_MERGE), _win_spec(layer, COL_MERGE + 1), _wbr_spec(layer, 0),
                  _layer_spec(layer, 1, GM_WIDTH), _const_spec(wbd.shape), _const_spec(bias.shape)],
        out_specs=out_specs,
        out_shape=out_shape,
        compiler_params=_params(resident + streamed + temps),
        name="gmlp_branch",
    )(h, w_in, w_in, w_in, w_in, w_in, w_branch, vnorm_g, wbd, bias)


def _hgrn2_kernel(h_ref, wq_ref, wf_ref, wi_ref, wo_ref, mg0_ref, mg1_ref, wbr_ref,
                  loglb_ref, log1mlb_ref, omlb_ref, og_ref, s0_ref,
                  out_ref, sn_ref,
                  qe_scr, qt_scr, kd_scr, kt_scr, v_scr, inter_scr, intra_scr, cq_scr, ck_scr, cv_scr, cg_scr,
                  *, grp):
    nb, tm, L = grp.nb, grp.tm, grp.hg_chunk
    n_chunks = tm // L
    n_sub = L // HG_SUB
    W = HG_WIDTH

    @pl.when(pl.program_id(1) == 0)
    def _():
        sn_ref[...] = s0_ref[...]

    hh = h_ref[...].reshape(tm, D_MODEL)

    def gates(rows_h):
        q = jax.nn.silu(_dot(rows_h, wq_ref[...]))
        zf = _dot(rows_h, wf_ref[...])
        log_f = jnp.logaddexp(loglb_ref[...], log1mlb_ref[...] + jax.nn.log_sigmoid(zf))
        k = omlb_ref[...] * jax.nn.sigmoid(-zf)
        return q, k, log_f, _dot(rows_h, wi_ref[...])

    q, k, log_f, v = gates(hh)
    shape4 = (n_chunks, n_sub, HG_SUB, W)
    g_sub = _scan_rows(None, log_f, HG_SUB).reshape(shape4)
    totals = g_sub[:, :, HG_SUB - 1:HG_SUB, :]
    refs = []
    acc = jnp.zeros((n_chunks, 1, 1, W), F32)
    for i in range(n_sub):
        refs.append(acc)
        acc = acc + totals[:, i:i + 1]
    g_last = acc
    G = g_sub + jnp.concatenate(refs, axis=1)
    q4, k4 = q.reshape(shape4), k.reshape(shape4)
    operand = lambda a: a.reshape(tm, W).astype(BF16)
    qe_scr[...] = operand(q4 * jnp.exp(G))
    qt_scr[...] = operand(q4 * jnp.exp(g_sub))
    kd_scr[...] = operand(k4 * jnp.exp(g_last - G))
    for i in range(n_sub):
        kt_scr[i] = operand(k4 * jnp.exp(jnp.minimum(refs[i] - G, HG_CLAMP)))
    v_scr[...] = v.astype(BF16)
    chunk_decay = jnp.exp(g_last).reshape(n_chunks, W)
    out_gate = jax.nn.silu(_dot(hh, wo_ref[...]))
    merge_gates = _merge_gates(hh, mg0_ref, mg1_ref)

    tril = lax.broadcasted_iota(jnp.int32, (L, L), 0) >= lax.broadcasted_iota(jnp.int32, (L, L), 1)
    heads = range(HG_HEADS)
    head_cols = [slice(hd * HG_DK, (hd + 1) * HG_DK) for hd in heads]
    chunk_rows = [slice(c * L, (c + 1) * L) for c in range(n_chunks)]
    contract_last = (((1,), (1,)), ((), ()))
    contract_rows = (((0,), (0,)), ((), ()))

    def score_rows(c, i, cols):
        seen = (i + 1) * HG_SUB
        a = lax.dot_general(qt_scr[c * L + i * HG_SUB:c * L + seen, cols], kt_scr[i, c * L:c * L + seen, cols],
                            contract_last, preferred_element_type=F32)
        return a if seen == L else jnp.concatenate([a, jnp.zeros((HG_SUB, L - seen), F32)], axis=1)

    for c, rows in enumerate(chunk_rows):
        scores = [jnp.concatenate([score_rows(c, i, cols) for i in range(n_sub)], axis=0) for cols in head_cols]
        scores = [jnp.where(tril, A, 0.0).astype(BF16) for A in scores]
        intra_scr[rows, :] = jnp.concatenate([_dot(A, v_scr[rows, cols]) for A, cols in zip(scores, head_cols)], axis=-1)
    updates = [[lax.dot_general(kd_scr[rows, cols], v_scr[rows, cols], contract_rows, preferred_element_type=F32)
                for cols in head_cols] for rows in chunk_rows]
    S = None
    for c, rows in enumerate(chunk_rows):
        slot = 0 if nb == 1 else c
        if nb > 1 or c == 0:
            S = [sn_ref[slot, hd] for hd in heads]
        inter_scr[rows, :] = jnp.concatenate(
            [_dot(qe_scr[rows, cols], S[hd].astype(BF16)) for hd, cols in enumerate(head_cols)], axis=-1)
        S = [jnp.transpose(jnp.broadcast_to(chunk_decay[c:c + 1, cols], (HG_DV, HG_DK))) * S[hd] + updates[c][hd]
             for hd, cols in enumerate(head_cols)]
        if nb > 1 or c == n_chunks - 1:
            for hd in heads:
                sn_ref[slot, hd] = S[hd]

    @pl.when(jnp.min(totals) < -HG_CLAMP)
    def _():
        t_idx = lax.broadcasted_iota(jnp.int32, (L, 1), 0)
        for c in range(n_chunks):
            rows = slice(c * L, (c + 1) * L)
            cq_scr[...], ck_scr[...], log_f_c, cv_scr[...] = gates(hh[rows])
            cg_scr[...] = _scan_rows(None, log_f_c, L)
            for hd in range(HG_HEADS):
                cols = slice(hd * HG_DK, (hd + 1) * HG_DK)
                G_c, q_c = cg_scr[:, cols], cq_scr[:, cols]

                def add_sources(blk, acc):
                    src = pl.ds(pl.multiple_of(blk * SUBLANES, SUBLANES), SUBLANES)
                    g_blk, k_blk, v_blk = cg_scr[src, cols], ck_scr[src, cols], cv_scr[src, cols]
                    for r in range(SUBLANES):
                        g_s, k_s, v_s = g_blk[r:r + 1, :], k_blk[r:r + 1, :], v_blk[r:r + 1, :]
                        w = jnp.sum(q_c * k_s * jnp.exp(jnp.minimum(G_c - g_s, 0.0)), axis=-1, keepdims=True)
                        acc = acc + jnp.where(t_idx >= blk * SUBLANES + r, w, 0.0) * v_s
                    return acc

                intra_scr[rows, cols] = lax.fori_loop(0, L // SUBLANES, add_sources, jnp.zeros((L, HG_DV), F32))

    o = inter_scr[...] + intra_scr[...]
    y = jnp.concatenate(
        [_rmsnorm(o[:, hd * HG_DV:(hd + 1) * HG_DV], og_ref[...]) for hd in range(HG_HEADS)], axis=-1)
    oc = _gated_out(merge_gates, (y * out_gate).astype(BF16), wbr_ref)
    out_ref[...] = oc.astype(BF16).reshape(grp.nb, grp.ts, D_MODEL)


def _hgrn2_call(grp, layer, h, w_in, w_branch, loglb, log1mlb, omlb, onorm_g, s0):
    tm, L = grp.tm, grp.hg_chunk
    n_sub = L // HG_SUB
    state = grp.nb * HG_HEADS * HG_DK * HG_DV * 4
    resident = (6 * D_MODEL * COL + COL * D_MODEL) * 2
    streamed = 2 * tm * (D_MODEL * 2 + D_MODEL * 2) + 3 * state
    scratch = (4 + n_sub) * tm * HG_WIDTH * 2 + 2 * tm * HG_WIDTH * 4 + 4 * L * HG_WIDTH * 4
    temps = 10 * tm * HG_WIDTH * 4
    tile16 = pltpu.VMEM((tm, HG_WIDTH), BF16)
    tile32 = pltpu.VMEM((tm, HG_WIDTH), F32)
    chunk32 = pltpu.VMEM((L, HG_WIDTH), F32)
    return pl.pallas_call(
        functools.partial(_hgrn2_kernel, grp=grp),
        grid=grp.grid,
        in_specs=[_tok_spec(grp, D_MODEL),
                  _win_spec(layer, COL_HG_Q), _win_spec(layer, COL_HG_F), _win_spec(layer, COL_HG_I),
                  _win_spec(layer, COL_HG_O),
                  _win_spec(layer, COL_MERGE + 4), _win_spec(layer, COL_MERGE + 5), _wbr_spec(layer, 2),
                  _layer_spec(layer, 1, HG_WIDTH), _layer_spec(layer, 1, HG_WIDTH), _layer_spec(layer, 1, HG_WIDTH),
                  _layer_spec(layer, 1, HG_DV),
                  _layer_seq_spec(grp, layer, HG_HEADS, HG_DK, HG_DV, pipeline_mode=pl.Buffered(1))],
        out_specs=[_tok_spec(grp, D_MODEL), _seq_spec(grp, HG_HEADS, HG_DK, HG_DV)],
        out_shape=[jax.ShapeDtypeStruct((grp.batch, grp.seq, D_MODEL), BF16),
                   jax.ShapeDtypeStruct((grp.batch, HG_HEADS, HG_DK, HG_DV), F32)],
        scratch_shapes=[tile16, tile16, tile16, pltpu.VMEM((n_sub, tm, HG_WIDTH), BF16), tile16,
                        tile32, tile32, chunk32, chunk32, chunk32, chunk32],
        compiler_params=_params(resident + streamed + scratch + temps),
        name="hgrn2_branch",
    )(h, w_in, w_in, w_in, w_in, w_in, w_in, w_branch, loglb, log1mlb, omlb, onorm_g, s0)


def _rglru_kernel(h_ref, wx_ref, wg_ref, mg0_ref, mg1_ref, wbr_ref, convw_ref, convb_ref, wa_ref, ba_ref,
                  wi_ref, bi_ref, sp_ref, conv0_ref, h0_ref, oa_ref, oc_ref, x_ref, wout_ref, gate_ref, g_ref,
                  *rest, grp, last):
    if last:
        convn_ref, hn_ref, y_ref, xbuf, hcar = rest
    else:
        scale_ref, shift_ref, convn_ref, hn_ref, xn_ref, hnext_ref, xbuf, hcar = rest
    nb, ts, tm = grp.nb, grp.ts, grp.tm
    hist = CONV_PAD - (LRU_CONV - 1)

    @pl.when(pl.program_id(1) == 0)
    def _():
        xbuf[:, hist:CONV_PAD, :] = conv0_ref[...]
        hcar[...] = h0_ref[...]

    hh = h_ref[...].reshape(tm, D_MODEL)
    xbuf[:, CONV_PAD:CONV_PAD + ts, :] = _dot(hh, wx_ref[...]).reshape(nb, ts, LRU_WIDTH)
    y = convb_ref[...]
    for tap in range(LRU_CONV):
        y = y + convw_ref[tap:tap + 1, :] * xbuf[:, hist + tap:hist + tap + ts, :]
    conv_new = xbuf[:, CONV_PAD + ts - (LRU_CONV - 1):CONV_PAD + ts, :]
    xbuf[:, hist:CONV_PAD, :] = conv_new
    convn_ref[...] = conv_new

    x = y.reshape(tm, LRU_WIDTH)
    xb = x.astype(BF16)
    width = LRU_WIDTH // LRU_BLOCKS

    def block_diag(w_ref, b_ref):
        return jnp.concatenate([_dot(xb[:, n * width:(n + 1) * width], w_ref[n]) for n in range(LRU_BLOCKS)],
                               axis=-1) + b_ref[...]

    gate_r = jax.nn.sigmoid(block_diag(wa_ref, ba_ref))
    gate_i = jax.nn.sigmoid(block_diag(wi_ref, bi_ref))
    log_a = -LRU_C * gate_r * sp_ref[...]
    a = jnp.exp(log_a)
    drive = jnp.sqrt(-jnp.tanh(log_a) * (a * a + 1.0)) * gate_i * x
    h0 = jnp.broadcast_to(hcar[...], (nb, ts, LRU_WIDTH)).reshape(tm, LRU_WIDTH)
    first = lax.broadcasted_iota(jnp.int32, (tm, 1), 0) % ts == 0
    drive = drive + jnp.where(first, a * h0, 0.0)
    hseq = _scan_rows(a, drive, ts)
    h_last = hseq.reshape(nb, ts, LRU_WIDTH)[:, ts - 1:ts, :]
    hcar[...] = h_last
    hn_ref[...] = h_last
    yb = hseq * jax.nn.silu(_dot(hh, wg_ref[...]))
    ob = _gated_out(_merge_gates(hh, mg0_ref, mg1_ref), yb.astype(BF16), wbr_ref)

    as_rows = lambda ref: ref[...].reshape(tm, D_MODEL).astype(F32)
    merged = (as_rows(oa_ref) + ob) + as_rows(oc_ref)
    out = _dot(merged.astype(BF16), wout_ref[...]).reshape(nb, ts, D_MODEL)
    xn = x_ref[...] + gate_ref[...] * out
    if last:
        y_ref[...] = _rmsnorm(xn, g_ref[...])
    else:
        xn_ref[...] = xn
        hnext_ref[...] = _adaln(xn, g_ref[...], scale_ref[...], shift_ref[...])


def _rglru_call(grp, layer, h, w_in, w_branch, w_out, conv_w, conv_b, wa, ba, wi, bi, sp, conv0, h0,
                oa, oc, x, gate, norm_g, scale=None, shift=None):
    last = scale is None
    tm = grp.tm
    tok = _tok_spec(grp, D_MODEL)
    in_specs = [tok,
                _win_spec(layer, COL_LRU_X), _win_spec(layer, COL_LRU_G),
                _win_spec(layer, COL_MERGE + 2), _win_spec(layer, COL_MERGE + 3), _wbr_spec(layer, 1),
                _layer_spec(layer, LRU_CONV, LRU_WIDTH), _layer_spec(layer, 1, LRU_WIDTH),
                _layer_spec(layer, LRU_BLOCKS, LANES, LANES), _layer_spec(layer, 1, LRU_WIDTH),
                _layer_spec(layer, LRU_BLOCKS, LANES, LANES), _layer_spec(layer, 1, LRU_WIDTH),
                _layer_spec(layer, 1, LRU_WIDTH),
                _layer_seq_spec(grp, layer, LRU_CONV - 1, LRU_WIDTH), _layer_seq_spec(grp, layer, 1, LRU_WIDTH),
                tok, tok, tok, _layer_spec(layer, D_MODEL, D_MODEL), _seq_spec(grp, 1, D_MODEL)]
    args = [h, w_in, w_in, w_in, w_in, w_branch, conv_w, conv_b, wa, ba, wi, bi, sp, conv0, h0,
            oa, oc, x, w_out, gate]
    out_specs = [_seq_spec(grp, LRU_CONV - 1, LRU_WIDTH), _seq_spec(grp, 1, LRU_WIDTH)]
    out_shape = [jax.ShapeDtypeStruct((grp.batch, LRU_CONV - 1, LRU_WIDTH), F32),
                 jax.ShapeDtypeStruct((grp.batch, 1, LRU_WIDTH), F32)]
    if last:
        in_specs.append(pl.BlockSpec((1, D_MODEL), lambda b, j: (0, 0)))
        args.append(norm_g)
        out_specs.append(tok)
        out_shape.append(jax.ShapeDtypeStruct(x.shape, F32))
    else:
        in_specs += [_layer_spec(layer + 1, 1, D_MODEL), _seq_spec(grp, 1, D_MODEL), _seq_spec(grp, 1, D_MODEL)]
        args += [norm_g, scale, shift]
        out_specs += [tok, tok]
        out_shape += [jax.ShapeDtypeStruct(x.shape, F32), jax.ShapeDtypeStruct(x.shape, BF16)]
    resident = (4 * D_MODEL * COL + COL * D_MODEL + D_MODEL * D_MODEL + 2 * LRU_WIDTH * LRU_WIDTH // LRU_BLOCKS) * 2
    streamed = 2 * tm * D_MODEL * (3 * 2 + 2 * 4 + 2)
    scratch = grp.nb * (CONV_PAD + grp.ts) * LRU_WIDTH * 4
    temps = 8 * tm * D_MODEL * 4
    return pl.pallas_call(
        functools.partial(_rglru_kernel, grp=grp, last=last),
        grid=grp.grid,
        in_specs=in_specs,
        out_specs=out_specs,
        out_shape=out_shape,
        scratch_shapes=[pltpu.VMEM((grp.nb, CONV_PAD + grp.ts, LRU_WIDTH), F32),
                        pltpu.VMEM((grp.nb, 1, LRU_WIDTH), F32)],
        compiler_params=_params(resident + streamed + scratch + temps),
        name="rglru_merge",
    )(*args)


def _gmlp_mix_weights(grp, ws, bs):
    L = grp.gm_chunk
    reps = grp.tm // L
    w = jnp.where(jnp.tril(jnp.ones((L, L), dtype=bool))[None], ws[:, :L, :L], 0.0)
    wbd = jnp.einsum('ab,gts->gatbs', jnp.eye(reps, dtype=F32), w).reshape(GM_GROUPS, grp.tm, grp.tm)
    bias = jnp.repeat(jnp.tile(bs[:, :L].T, (reps, 1)), GM_WIDTH // GM_GROUPS, axis=1)
    return wbd.astype(BF16), bias


def _run_group(grp, x, mod, conv0, h0, s0, want_v, weights):
    (norm_g, w_in, gm_vnorm_g, gm_ws, gm_bs, conv_w, conv_b, wa, ba, wi, bi, sp,
     loglb, log1mlb, omlb, onorm_g, w_branch, w_out, final_g) = weights
    shift = mod[:, :, None, 0:D_MODEL]
    scale = mod[:, :, None, D_MODEL:2 * D_MODEL]
    gate = mod[:, :, None, 2 * D_MODEL:]
    h = _prenorm_call(_Group(grp.batch, grp.seq, NORM_TILE_ROWS), 0, x, norm_g, scale[0], shift[0])
    gm_grp = _Group(grp.batch, grp.seq, GM_TILE_ROWS) if grp.nb == 1 else grp
    hg_grp = grp if grp.nb == 1 else _Group(grp.batch, grp.seq, HG_MULTISEQ_ROWS)
    convs, hs, ss, vs = [], [], [], []
    y = None
    for l in range(DEPTH):
        wbd, bias = _gmlp_mix_weights(gm_grp, gm_ws[l], gm_bs[l])
        res = _gmlp_call(gm_grp, l, h, w_in, w_branch, gm_vnorm_g, wbd, bias, want_v)
        oa = res[0]
        if want_v:
            vs.append(res[1])
        oc, s_n = _hgrn2_call(hg_grp, l, h, w_in, w_branch, loglb, log1mlb, omlb, onorm_g, s0)
        lru = (grp, l, h, w_in, w_branch, w_out, conv_w, conv_b, wa, ba, wi, bi, sp, conv0, h0, oa, oc, x, gate[l])
        if l + 1 < DEPTH:
            conv_n, h_n, x, h = _rglru_call(*lru, norm_g, scale[l + 1], shift[l + 1])
        else:
            conv_n, h_n, y = _rglru_call(*lru, final_g)
        convs.append(conv_n)
        hs.append(h_n[:, 0, :])
        ss.append(s_n)
    return y, jnp.stack(convs), jnp.stack(hs), jnp.stack(ss), (jnp.stack(vs) if want_v else None)


def kernel(x_prompt, x_sample, c_prompt, c_sample, state_rglru_conv, state_rglru_h, state_hgrn2,
           w_ada, b_ada, norm_g, w_in, gm_vnorm_g, gm_ws, gm_bs, lru_conv_w, lru_conv_b,
           lru_wa, lru_ba, lru_wx, lru_bx, lru_lambda, hg_lb, hg_onorm_g, w_branch, w_out, final_g):
    batch_p, seq_p, _ = x_prompt.shape
    batch_s, seq_s, _ = x_sample.shape
    grp_p, grp_s = _Group(batch_p, seq_p), _Group(batch_s, seq_s)

    p = jax.nn.softmax(hg_lb.astype(F32), axis=0)
    cs = jnp.cumsum(p, axis=0)
    lbs = cs - cs[0]
    row = lambda a: a.reshape(DEPTH, 1, a.shape[-1])
    weights = (row(norm_g), w_in.astype(BF16), row(gm_vnorm_g), gm_ws, gm_bs,
               lru_conv_w, row(lru_conv_b), lru_wa.astype(BF16), row(lru_ba), lru_wx.astype(BF16), row(lru_bx),
               row(jax.nn.softplus(-lru_lambda.astype(F32))),
               row(jnp.log(lbs)), row(jnp.log1p(-lbs)), row(1.0 - lbs), row(hg_onorm_g),
               w_branch.astype(BF16), w_out.astype(BF16), final_g.reshape(1, D_MODEL))

    mod = _mod_call(jnp.concatenate([c_prompt, c_sample], axis=0), w_ada, b_ada)
    dt = x_prompt.dtype
    y_p, conv_p, h_p, s_p, _ = _run_group(
        grp_p, x_prompt, mod[:, :batch_p],
        jnp.zeros((DEPTH, batch_p, LRU_CONV - 1, LRU_WIDTH), dt),
        jnp.zeros((DEPTH, batch_p, 1, LRU_WIDTH), dt),
        jnp.zeros((DEPTH, batch_p, HG_HEADS, HG_DK, HG_DV), dt), False, weights)
    y_s, conv_s, h_s, s_s, v_s = _run_group(
        grp_s, x_sample, mod[:, batch_p:],
        state_rglru_conv, state_rglru_h[:, :, None, :], state_hgrn2, True, weights)
    return (y_p, y_s, conv_p, h_p, s_p, conv_s, h_s, s_s, v_s)
```

```python
import functools

import jax
import jax.numpy as jnp
from jax import lax
from jax.experimental import pallas as pl
from jax.experimental.pallas import tpu as pltpu

F32 = jnp.float32
BF16 = jnp.bfloat16

D_MODEL = 2048
DEPTH = 4
GM_CHUNK = 128
GM_GROUPS = 8
GM_WIDTH = 1024
LRU_WIDTH = 1024
LRU_BLOCKS = 8
LRU_CONV = 4
LRU_C = 8.0
HG_CHUNK = 64
HG_HEADS = 8
HG_DK = 128
HG_DV = 128
HG_WIDTH = HG_HEADS * HG_DV
EPS = 1e-6

LANES = 128
SUBLANES = 8
COL = 1024
COL_GM_U, COL_GM_V, COL_GM_G = 0, 1, 2
COL_LRU_X, COL_LRU_G = 3, 4
COL_HG_Q, COL_HG_F, COL_HG_I, COL_HG_O = 5, 6, 7, 8
COL_MERGE = 9
TILE_ROWS = 256
NORM_TILE_ROWS = 1024
HG_MULTISEQ_ROWS = 128
HG_SUB = 16
HG_CLAMP = 60.0
V7X_VMEM_BYTES = 64 * 1024 * 1024


class _Group:
    def __init__(self, batch, seq, tile_rows=TILE_ROWS):
        self.batch, self.seq = batch, seq
        self.ts = min(seq, tile_rows)
        self.nb = min(tile_rows // self.ts, batch)
        assert batch % self.nb == 0 and seq % self.ts == 0
        self.tm = self.nb * self.ts
        self.grid = (batch // self.nb, seq // self.ts)
        self.gm_chunk = min(seq, GM_CHUNK)
        self.hg_chunk = min(seq, HG_CHUNK)
        assert self.ts % self.gm_chunk == 0 and self.ts % self.hg_chunk == 0
        assert self.hg_chunk % HG_SUB == 0
        assert self.nb == 1 or self.grid[1] == 1


def _dot(a, b):
    return jnp.dot(a, b, preferred_element_type=F32)


def _rmsnorm(x, g):
    return x * lax.rsqrt(jnp.mean(x * x, axis=-1, keepdims=True) + EPS) * g


def _adaln(x, g, scale, shift):
    return (_rmsnorm(x, g) * (1.0 + scale) + shift).astype(BF16)


def _tok_spec(grp, width):
    return pl.BlockSpec((grp.nb, grp.ts, width), lambda b, j: (b, j, 0))


def _seq_spec(grp, *tail):
    zeros = (0,) * len(tail)
    return pl.BlockSpec((grp.nb,) + tail, lambda b, j: (b,) + zeros)


def _layer_seq_spec(grp, layer, *tail, **kwargs):
    zeros = (0,) * len(tail)
    return pl.BlockSpec((None, grp.nb) + tail, lambda b, j: (layer, b) + zeros, **kwargs)


def _const_spec(shape):
    nd = len(shape)
    return pl.BlockSpec(shape, lambda b, j: (0,) * nd, pipeline_mode=pl.Buffered(1))


def _layer_spec(layer, *tail):
    zeros = (0,) * len(tail)
    return pl.BlockSpec((None,) + tail, lambda b, j: (layer,) + zeros, pipeline_mode=pl.Buffered(1))


def _win_spec(layer, col):
    return pl.BlockSpec((None, D_MODEL, COL), lambda b, j: (layer, 0, col), pipeline_mode=pl.Buffered(1))


def _wbr_spec(layer, branch):
    return pl.BlockSpec((None, COL, D_MODEL), lambda b, j: (layer, branch, 0), pipeline_mode=pl.Buffered(1))


def _params(vmem_bytes):
    return pltpu.CompilerParams(dimension_semantics=("arbitrary", "arbitrary"),
                                vmem_limit_bytes=min(int(vmem_bytes), V7X_VMEM_BYTES - (4 << 20)))


def _sigmoid(x):
    return 0.5 * jnp.tanh(0.5 * x) + 0.5


def _silu(x):
    return x * _sigmoid(x)


def _merge_gates(hh, mg0_ref, mg1_ref):
    return [_sigmoid(_dot(hh, mg_ref[...])) for mg_ref in (mg0_ref, mg1_ref)]


def _gated_out(gates, y, wbr_ref):
    return jnp.concatenate(
        [g * _dot(y, wbr_ref[:, half * COL:(half + 1) * COL]) for half, g in enumerate(gates)], axis=-1)


def _scan_rows(a, b, seg):
    rows, width = b.shape
    groups = rows // SUBLANES
    b3 = b.reshape(groups, SUBLANES, width)
    a3 = None if a is None else a.reshape(groups, SUBLANES, width)
    pos = lax.broadcasted_iota(jnp.int32, (1, SUBLANES, 1), 1)
    d = 1
    while d < SUBLANES:
        keep = pos >= d
        b_prev = pltpu.roll(b3, d, 1)
        if a3 is None:
            b3 = jnp.where(keep, b3 + b_prev, b3)
        else:
            b3 = jnp.where(keep, a3 * b_prev + b3, b3)
            a3 = jnp.where(keep, a3 * pltpu.roll(a3, d, 1), a3)
        d *= 2
    per_seg = seg // SUBLANES
    out = []
    for g in range(groups):
        hg = b3[g]
        if g % per_seg:
            carry = out[-1][SUBLANES - 1:SUBLANES, :]
            hg = hg + carry if a3 is None else a3[g] * carry + hg
        out.append(hg)
    return jnp.concatenate(out, axis=0)


def _mod_kernel(c_ref, w_ref, b_ref, o_ref):
    sc = jax.nn.silu(c_ref[...]).astype(BF16)
    o_ref[...] = _dot(sc, w_ref[...].astype(BF16)) + b_ref[...]


def _mod_call(c_all, w_ada, b_ada):
    rows = c_all.shape[0]
    width = 3 * D_MODEL
    tn = 1024
    return pl.pallas_call(
        _mod_kernel,
        grid=(DEPTH, width // tn),
        in_specs=[pl.BlockSpec((rows, D_MODEL), lambda l, n: (0, 0)),
                  pl.BlockSpec((None, D_MODEL, tn), lambda l, n: (l, 0, n)),
                  pl.BlockSpec((None, 1, tn), lambda l, n: (l, 0, n))],
        out_specs=pl.BlockSpec((None, rows, tn), lambda l, n: (l, 0, n)),
        out_shape=jax.ShapeDtypeStruct((DEPTH, rows, width), F32),
        compiler_params=_params(2 * D_MODEL * tn * 4 + D_MODEL * tn * 2 + (8 << 20)),
        name="adaln_mod",
    )(c_all, w_ada, b_ada.reshape(DEPTH, 1, width))


def _prenorm_kernel(x_ref, g_ref, scale_ref, shift_ref, h_ref):
    h_ref[...] = _adaln(x_ref[...], g_ref[...], scale_ref[...], shift_ref[...])


def _prenorm_call(grp, layer, x, norm_g, scale, shift):
    return pl.pallas_call(
        _prenorm_kernel,
        grid=grp.grid,
        in_specs=[_tok_spec(grp, D_MODEL), _layer_spec(layer, 1, D_MODEL),
                  _seq_spec(grp, 1, D_MODEL), _seq_spec(grp, 1, D_MODEL)],
        out_specs=_tok_spec(grp, D_MODEL),
        out_shape=jax.ShapeDtypeStruct(x.shape, BF16),
        compiler_params=_params(grp.tm * D_MODEL * (2 * 4 + 2 * 2 + 3 * 4) + (4 << 20)),
        name="prenorm",
    )(x, norm_g, scale, shift)


def _gmlp_kernel(h_ref, wu_ref, wv_ref, wg_ref, mg0_ref, mg1_ref, wbr_ref, vg_ref, wbd_ref, bias_ref,
                 out_ref, *v_out, grp):
    hh = h_ref[...].reshape(grp.tm, D_MODEL)
    v = _rmsnorm(jax.nn.gelu(_dot(hh, wv_ref[...])), vg_ref[...])
    if v_out:
        v_out[0][...] = v.reshape(grp.nb, grp.ts, GM_WIDTH)
    vb = v.astype(BF16)
    width = GM_WIDTH // GM_GROUPS
    s = jnp.concatenate([_dot(wbd_ref[g], vb[:, g * width:(g + 1) * width]) for g in range(GM_GROUPS)],
                        axis=-1) + bias_ref[...]
    u = jax.nn.gelu(_dot(hh, wu_ref[...]))
    ya = (u * s) * _silu(_dot(hh, wg_ref[...]))
    oa = _gated_out(_merge_gates(hh, mg0_ref, mg1_ref), ya.astype(BF16), wbr_ref)
    out_ref[...] = oa.astype(BF16).reshape(grp.nb, grp.ts, D_MODEL)


def _gmlp_call(grp, layer, h, w_in, w_branch, vnorm_g, wbd, bias, want_v):
    tm = grp.tm
    out_shape = [jax.ShapeDtypeStruct((grp.batch, grp.seq, D_MODEL), BF16)]
    out_specs = [_tok_spec(grp, D_MODEL)]
    if want_v:
        out_shape.append(jax.ShapeDtypeStruct((grp.batch, grp.seq, GM_WIDTH), F32))
        out_specs.append(_tok_spec(grp, GM_WIDTH))
    resident = (5 * D_MODEL * COL + COL * D_MODEL + GM_GROUPS * tm * tm) * 2 + tm * GM_WIDTH * 4
    streamed = 2 * tm * (D_MODEL * 2 + D_MODEL * 2 + GM_WIDTH * 4)
    temps = 12 * tm * GM_WIDTH * 4
    return pl.pallas_call(
        functools.partial(_gmlp_kernel, grp=grp),
        grid=grp.grid,
        in_specs=[_tok_spec(grp, D_MODEL),
                  _win_spec(layer, COL_GM_U), _win_spec(layer, COL_GM_V), _win_spec(layer, COL_GM_G),
                  _win_spec(layer, COL_MERGE), _win_spec(layer, COL_MERGE + 1), _wbr_spec(layer, 0),
                  _layer_spec(layer, 1, GM_WIDTH), _const_spec(wbd.shape), _const_spec(bias.shape)],
        out_specs=out_specs,
        out_shape=out_shape,
        compiler_params=_params(resident + streamed + temps),
        name="gmlp_branch",
    )(h, w_in, w_in, w_in, w_in, w_in, w_branch, vnorm_g, wbd, bias)


def _hgrn2_kernel(h_ref, wq_ref, wf_ref, wi_ref, wo_ref, mg0_ref, mg1_ref, wbr_ref,
                  loglb_ref, log1mlb_ref, omlb_ref, og_ref, s0_ref,
                  out_ref, sn_ref,
                  qe_scr, qt_scr, kd_scr, kt_scr, v_scr, inter_scr, intra_scr, cq_scr, ck_scr, cv_scr, cg_scr,
                  *, grp):
    nb, tm, L = grp.nb, grp.tm, grp.hg_chunk
    n_chunks = tm // L
    n_sub = L // HG_SUB
    W = HG_WIDTH

    @pl.when(pl.program_id(1) == 0)
    def _():
        sn_ref[...] = s0_ref[...]

    hh = h_ref[...].reshape(tm, D_MODEL)

    def gates(rows_h):
        q = _silu(_dot(rows_h, wq_ref[...]))
        zf = _dot(rows_h, wf_ref[...])
        magnitude = jnp.abs(zf)
        e = jnp.exp(-magnitude)
        r = _sigmoid(magnitude)
        k = omlb_ref[...] * jnp.where(zf >= 0.0, e * r, r)
        log_sigmoid = jnp.minimum(zf, 0.0) + jnp.log(r)
        a, b = loglb_ref[...], log1mlb_ref[...] + log_sigmoid
        log_f = jnp.maximum(a, b) + jnp.log(1.0 + jnp.exp(-jnp.abs(a - b)))
        return q, k, log_f, _dot(rows_h, wi_ref[...])

    q, k, log_f, v = gates(hh)
    shape4 = (n_chunks, n_sub, HG_SUB, W)
    g_sub = _scan_rows(None, log_f, HG_SUB).reshape(shape4)
    totals = g_sub[:, :, HG_SUB - 1:HG_SUB, :]
    refs = []
    acc = jnp.zeros((n_chunks, 1, 1, W), F32)
    for i in range(n_sub):
        refs.append(acc)
        acc = acc + totals[:, i:i + 1]
    g_last = acc
    G = g_sub + jnp.concatenate(refs, axis=1)
    q4, k4 = q.reshape(shape4), k.reshape(shape4)
    operand = lambda a: a.reshape(tm, W).astype(BF16)
    qe_scr[...] = operand(q4 * jnp.exp(G))
    qt_scr[...] = operand(q4 * jnp.exp(g_sub))
    kd_scr[...] = operand(k4 * jnp.exp(g_last - G))
    for i in range(n_sub):
        kt_scr[i] = operand(k4 * jnp.exp(jnp.minimum(refs[i] - G, HG_CLAMP)))
    v_scr[...] = v.astype(BF16)
    chunk_decay = jnp.exp(g_last).reshape(n_chunks, W)
    out_gate = _silu(_dot(hh, wo_ref[...]))
    merge_gates = _merge_gates(hh, mg0_ref, mg1_ref)

    tril = lax.broadcasted_iota(jnp.int32, (L, L), 0) >= lax.broadcasted_iota(jnp.int32, (L, L), 1)
    heads = range(HG_HEADS)
    head_cols = [slice(hd * HG_DK, (hd + 1) * HG_DK) for hd in heads]
    chunk_rows = [slice(c * L, (c + 1) * L) for c in range(n_chunks)]
    contract_last = (((1,), (1,)), ((), ()))
    contract_rows = (((0,), (0,)), ((), ()))

    def score_rows(c, i, cols):
        seen = (i + 1) * HG_SUB
        a = lax.dot_general(qt_scr[c * L + i * HG_SUB:c * L + seen, cols], kt_scr[i, c * L:c * L + seen, cols],
                            contract_last, preferred_element_type=F32)
        return a if seen == L else jnp.concatenate([a, jnp.zeros((HG_SUB, L - seen), F32)], axis=1)

    for c, rows in enumerate(chunk_rows):
        scores = [jnp.concatenate([score_rows(c, i, cols) for i in range(n_sub)], axis=0) for cols in head_cols]
        scores = [jnp.where(tril, A, 0.0).astype(BF16) for A in scores]
        intra_scr[rows, :] = jnp.concatenate([_dot(A, v_scr[rows, cols]) for A, cols in zip(scores, head_cols)], axis=-1)
    updates = [[lax.dot_general(kd_scr[rows, cols], v_scr[rows, cols], contract_rows, preferred_element_type=F32)
                for cols in head_cols] for rows in chunk_rows]
    S = None
    for c, rows in enumerate(chunk_rows):
        slot = 0 if nb == 1 else c
        if nb > 1 or c == 0:
            S = [sn_ref[slot, hd] for hd in heads]
        inter_scr[rows, :] = jnp.concatenate(
            [_dot(qe_scr[rows, cols], S[hd].astype(BF16)) for hd, cols in enumerate(head_cols)], axis=-1)
        S = [jnp.transpose(jnp.broadcast_to(chunk_decay[c:c + 1, cols], (HG_DV, HG_DK))) * S[hd] + updates[c][hd]
             for hd, cols in enumerate(head_cols)]
        if nb > 1 or c == n_chunks - 1:
            for hd in heads:
                sn_ref[slot, hd] = S[hd]

    @pl.when(jnp.min(totals) < -HG_CLAMP)
    def _():
        t_idx = lax.broadcasted_iota(jnp.int32, (L, 1), 0)
        for c in range(n_chunks):
            rows = slice(c * L, (c + 1) * L)
            cq_scr[...], ck_scr[...], log_f_c, cv_scr[...] = gates(hh[rows])
            cg_scr[...] = _scan_rows(None, log_f_c, L)
            for hd in range(HG_HEADS):
                cols = slice(hd * HG_DK, (hd + 1) * HG_DK)
                G_c, q_c = cg_scr[:, cols], cq_scr[:, cols]

                def add_sources(blk, acc):
                    src = pl.ds(pl.multiple_of(blk * SUBLANES, SUBLANES), SUBLANES)
                    g_blk, k_blk, v_blk = cg_scr[src, cols], ck_scr[src, cols], cv_scr[src, cols]
                    for r in range(SUBLANES):
                        g_s, k_s, v_s = g_blk[r:r + 1, :], k_blk[r:r + 1, :], v_blk[r:r + 1, :]
                        w = jnp.sum(q_c * k_s * jnp.exp(jnp.minimum(G_c - g_s, 0.0)), axis=-1, keepdims=True)
                        acc = acc + jnp.where(t_idx >= blk * SUBLANES + r, w, 0.0) * v_s
                    return acc

                intra_scr[rows, cols] = lax.fori_loop(0, L // SUBLANES, add_sources, jnp.zeros((L, HG_DV), F32))

    o = inter_scr[...] + intra_scr[...]
    y = jnp.concatenate(
        [_rmsnorm(o[:, hd * HG_DV:(hd + 1) * HG_DV], og_ref[...]) for hd in range(HG_HEADS)], axis=-1)
    oc = _gated_out(merge_gates, (y * out_gate).astype(BF16), wbr_ref)
    out_ref[...] = oc.astype(BF16).reshape(grp.nb, grp.ts, D_MODEL)


def _hgrn2_call(grp, layer, h, w_in, w_branch, loglb, log1mlb, omlb, onorm_g, s0):
    tm, L = grp.tm, grp.hg_chunk
    n_sub = L // HG_SUB
    state = grp.nb * HG_HEADS * HG_DK * HG_DV * 4
    resident = (6 * D_MODEL * COL + COL * D_MODEL) * 2
    streamed = 2 * tm * (D_MODEL * 2 + D_MODEL * 2) + 3 * state
    scratch = (4 + n_sub) * tm * HG_WIDTH * 2 + 2 * tm * HG_WIDTH * 4 + 4 * L * HG_WIDTH * 4
    temps = 10 * tm * HG_WIDTH * 4
    tile16 = pltpu.VMEM((tm, HG_WIDTH), BF16)
    tile32 = pltpu.VMEM((tm, HG_WIDTH), F32)
    chunk32 = pltpu.VMEM((L, HG_WIDTH), F32)
    return pl.pallas_call(
        functools.partial(_hgrn2_kernel, grp=grp),
        grid=grp.grid,
        in_specs=[_tok_spec(grp, D_MODEL),
                  _win_spec(layer, COL_HG_Q), _win_spec(layer, COL_HG_F), _win_spec(layer, COL_HG_I),
                  _win_spec(layer, COL_HG_O),
                  _win_spec(layer, COL_MERGE + 4), _win_spec(layer, COL_MERGE + 5), _wbr_spec(layer, 2),
                  _layer_spec(layer, 1, HG_WIDTH), _layer_spec(layer, 1, HG_WIDTH), _layer_spec(layer, 1, HG_WIDTH),
                  _layer_spec(layer, 1, HG_DV),
                  _layer_seq_spec(grp, layer, HG_HEADS, HG_DK, HG_DV, pipeline_mode=pl.Buffered(1))],
        out_specs=[_tok_spec(grp, D_MODEL), _seq_spec(grp, HG_HEADS, HG_DK, HG_DV)],
        out_shape=[jax.ShapeDtypeStruct((grp.batch, grp.seq, D_MODEL), BF16),
                   jax.ShapeDtypeStruct((grp.batch, HG_HEADS, HG_DK, HG_DV), F32)],
        scratch_shapes=[tile16, tile16, tile16, pltpu.VMEM((n_sub, tm, HG_WIDTH), BF16), tile16,
                        tile32, tile32, chunk32, chunk32, chunk32, chunk32],
        compiler_params=_params(resident + streamed + scratch + temps),
        name="hgrn2_branch",
    )(h, w_in, w_in, w_in, w_in, w_in, w_in, w_branch, loglb, log1mlb, omlb, onorm_g, s0)


def _rglru_kernel(h_ref, wx_ref, wg_ref, mg0_ref, mg1_ref, wbr_ref, convw_ref, convb_ref, wa_ref, ba_ref,
                  wi_ref, bi_ref, sp_ref, conv0_ref, h0_ref, oa_ref, oc_ref, x_ref, wout_ref, gate_ref, g_ref,
                  *rest, grp, last):
    if last:
        convn_ref, hn_ref, y_ref, tail_scr, hcar = rest
    else:
        scale_ref, shift_ref, convn_ref, hn_ref, xn_ref, hnext_ref, tail_scr, hcar = rest
    nb, ts, tm = grp.nb, grp.ts, grp.tm
    groups = ts // SUBLANES
    history = SUBLANES - (LRU_CONV - 1)

    @pl.when(pl.program_id(1) == 0)
    def _():
        tail_scr[...] = jnp.zeros(tail_scr.shape, F32)
        tail_scr[:, history:, :] = conv0_ref[...]
        hcar[...] = h0_ref[...]

    hh = h_ref[...].reshape(tm, D_MODEL)
    lx = _dot(hh, wx_ref[...]).reshape(nb, groups, SUBLANES, LRU_WIDTH)
    ext = jnp.concatenate([tail_scr[...].reshape(nb, 1, SUBLANES, LRU_WIDTH), lx], axis=1)
    pos = lax.broadcasted_iota(jnp.int32, (1, 1, SUBLANES, 1), 2)
    y = convb_ref[...]
    for tap in range(LRU_CONV):
        shift = LRU_CONV - 1 - tap
        if shift:
            rolled = pltpu.roll(ext.reshape(nb * (groups + 1), SUBLANES, LRU_WIDTH), shift, 1)
            rolled = rolled.reshape(nb, groups + 1, SUBLANES, LRU_WIDTH)
            shifted = jnp.where(pos >= shift, rolled[:, 1:], rolled[:, :groups])
        else:
            shifted = lx
        y = y + convw_ref[tap:tap + 1, :] * shifted
    tail = lx[:, groups - 1]
    tail_scr[...] = tail
    convn_ref[...] = tail[:, history:, :]

    x = y.reshape(tm, LRU_WIDTH)
    xb = x.astype(BF16)
    width = LRU_WIDTH // LRU_BLOCKS

    def block_diag(w_ref, b_ref):
        return jnp.concatenate([_dot(xb[:, n * width:(n + 1) * width], w_ref[n]) for n in range(LRU_BLOCKS)],
                               axis=-1) + b_ref[...]

    gate_r = _sigmoid(block_diag(wa_ref, ba_ref))
    gate_i = _sigmoid(block_diag(wi_ref, bi_ref))
    log_a = -LRU_C * gate_r * sp_ref[...]
    a = jnp.exp(log_a)
    drive = jnp.sqrt(-jnp.tanh(log_a) * (a * a + 1.0)) * gate_i * x
    h0 = jnp.broadcast_to(hcar[...], (nb, ts, LRU_WIDTH)).reshape(tm, LRU_WIDTH)
    first = lax.broadcasted_iota(jnp.int32, (tm, 1), 0) % ts == 0
    drive = drive + jnp.where(first, a * h0, 0.0)
    hseq = _scan_rows(a, drive, ts)
    h_last = hseq.reshape(nb, ts, LRU_WIDTH)[:, ts - 1:ts, :]
    hcar[...] = h_last
    hn_ref[...] = h_last
    yb = hseq * _silu(_dot(hh, wg_ref[...]))
    ob = _gated_out(_merge_gates(hh, mg0_ref, mg1_ref), yb.astype(BF16), wbr_ref)

    as_rows = lambda ref: ref[...].reshape(tm, D_MODEL).astype(F32)
    merged = (as_rows(oa_ref) + ob) + as_rows(oc_ref)
    out = _dot(merged.astype(BF16), wout_ref[...]).reshape(nb, ts, D_MODEL)
    xn = x_ref[...] + gate_ref[...] * out
    if last:
        y_ref[...] = _rmsnorm(xn, g_ref[...])
    else:
        xn_ref[...] = xn
        hnext_ref[...] = _adaln(xn, g_ref[...], scale_ref[...], shift_ref[...])


def _rglru_call(grp, layer, h, w_in, w_branch, w_out, conv_w, conv_b, wa, ba, wi, bi, sp, conv0, h0,
                oa, oc, x, gate, norm_g, scale=None, shift=None):
    last = scale is None
    tm = grp.tm
    tok = _tok_spec(grp, D_MODEL)
    in_specs = [tok,
                _win_spec(layer, COL_LRU_X), _win_spec(layer, COL_LRU_G),
                _win_spec(layer, COL_MERGE + 2), _win_spec(layer, COL_MERGE + 3), _wbr_spec(layer, 1),
                _layer_spec(layer, LRU_CONV, LRU_WIDTH), _layer_spec(layer, 1, LRU_WIDTH),
                _layer_spec(layer, LRU_BLOCKS, LANES, LANES), _layer_spec(layer, 1, LRU_WIDTH),
                _layer_spec(layer, LRU_BLOCKS, LANES, LANES), _layer_spec(layer, 1, LRU_WIDTH),
                _layer_spec(layer, 1, LRU_WIDTH),
                _layer_seq_spec(grp, layer, LRU_CONV - 1, LRU_WIDTH), _layer_seq_spec(grp, layer, 1, LRU_WIDTH),
                tok, tok, tok, _layer_spec(layer, D_MODEL, D_MODEL), _seq_spec(grp, 1, D_MODEL)]
    args = [h, w_in, w_in, w_in, w_in, w_branch, conv_w, conv_b, wa, ba, wi, bi, sp, conv0, h0,
            oa, oc, x, w_out, gate]
    out_specs = [_seq_spec(grp, LRU_CONV - 1, LRU_WIDTH), _seq_spec(grp, 1, LRU_WIDTH)]
    out_shape = [jax.ShapeDtypeStruct((grp.batch, LRU_CONV - 1, LRU_WIDTH), F32),
                 jax.ShapeDtypeStruct((grp.batch, 1, LRU_WIDTH), F32)]
    if last:
        in_specs.append(pl.BlockSpec((1, D_MODEL), lambda b, j: (0, 0)))
        args.append(norm_g)
        out_specs.append(tok)
        out_shape.append(jax.ShapeDtypeStruct(x.shape, F32))
    else:
        in_specs += [_layer_spec(layer + 1, 1, D_MODEL), _seq_spec(grp, 1, D_MODEL), _seq_spec(grp, 1, D_MODEL)]
        args += [norm_g, scale, shift]
        out_specs += [tok, tok]
        out_shape += [jax.ShapeDtypeStruct(x.shape, F32), jax.ShapeDtypeStruct(x.shape, BF16)]
    resident = (4 * D_MODEL * COL + COL * D_MODEL + D_MODEL * D_MODEL + 2 * LRU_WIDTH * LRU_WIDTH // LRU_BLOCKS) * 2
    streamed = 2 * tm * D_MODEL * (3 * 2 + 2 * 4 + 2)
    scratch = grp.nb * SUBLANES * LRU_WIDTH * 4
    temps = 8 * tm * D_MODEL * 4
    return pl.pallas_call(
        functools.partial(_rglru_kernel, grp=grp, last=last),
        grid=grp.grid,
        in_specs=in_specs,
        out_specs=out_specs,
        out_shape=out_shape,
        scratch_shapes=[pltpu.VMEM((grp.nb, SUBLANES, LRU_WIDTH), F32),
                        pltpu.VMEM((grp.nb, 1, LRU_WIDTH), F32)],
        compiler_params=_params(resident + streamed + scratch + temps),
        name="rglru_merge",
    )(*args)


def _gmlp_mix_weights(grp, ws, bs):
    L = grp.gm_chunk
    reps = grp.tm // L
    w = jnp.where(jnp.tril(jnp.ones((L, L), dtype=bool))[None], ws[:, :L, :L], 0.0)
    wbd = jnp.einsum('ab,gts->gatbs', jnp.eye(reps, dtype=F32), w).reshape(GM_GROUPS, grp.tm, grp.tm)
    bias = jnp.repeat(jnp.tile(bs[:, :L].T, (reps, 1)), GM_WIDTH // GM_GROUPS, axis=1)
    return wbd.astype(BF16), bias


def _run_group(grp, x, mod, conv0, h0, s0, want_v, weights):
    (norm_g, w_in, gm_vnorm_g, gm_ws, gm_bs, conv_w, conv_b, wa, ba, wi, bi, sp,
     loglb, log1mlb, omlb, onorm_g, w_branch, w_out, final_g) = weights
    shift = mod[:, :, None, 0:D_MODEL]
    scale = mod[:, :, None, D_MODEL:2 * D_MODEL]
    gate = mod[:, :, None, 2 * D_MODEL:]
    h = _prenorm_call(_Group(grp.batch, grp.seq, NORM_TILE_ROWS), 0, x, norm_g, scale[0], shift[0])
    hg_grp = grp if grp.nb == 1 else _Group(grp.batch, grp.seq, HG_MULTISEQ_ROWS)
    convs, hs, ss, vs = [], [], [], []
    y = None
    for l in range(DEPTH):
        wbd, bias = _gmlp_mix_weights(grp, gm_ws[l], gm_bs[l])
        res = _gmlp_call(grp, l, h, w_in, w_branch, gm_vnorm_g, wbd, bias, want_v)
        oa = res[0]
        if want_v:
            vs.append(res[1])
        oc, s_n = _hgrn2_call(hg_grp, l, h, w_in, w_branch, loglb, log1mlb, omlb, onorm_g, s0)
        lru = (grp, l, h, w_in, w_branch, w_out, conv_w, conv_b, wa, ba, wi, bi, sp, conv0, h0, oa, oc, x, gate[l])
        if l + 1 < DEPTH:
            conv_n, h_n, x, h = _rglru_call(*lru, norm_g, scale[l + 1], shift[l + 1])
        else:
            conv_n, h_n, y = _rglru_call(*lru, final_g)
        convs.append(conv_n)
        hs.append(h_n[:, 0, :])
        ss.append(s_n)
    return y, jnp.stack(convs), jnp.stack(hs), jnp.stack(ss), (jnp.stack(vs) if want_v else None)


def kernel(x_prompt, x_sample, c_prompt, c_sample, state_rglru_conv, state_rglru_h, state_hgrn2,
           w_ada, b_ada, norm_g, w_in, gm_vnorm_g, gm_ws, gm_bs, lru_conv_w, lru_conv_b,
           lru_wa, lru_ba, lru_wx, lru_bx, lru_lambda, hg_lb, hg_onorm_g, w_branch, w_out, final_g):
    batch_p, seq_p, _ = x_prompt.shape
    batch_s, seq_s, _ = x_sample.shape
    grp_p, grp_s = _Group(batch_p, seq_p), _Group(batch_s, seq_s)

    p = jax.nn.softmax(hg_lb.astype(F32), axis=0)
    cs = jnp.cumsum(p, axis=0)
    lbs = cs - cs[0]
    row = lambda a: a.reshape(DEPTH, 1, a.shape[-1])
    weights = (row(norm_g), w_in.astype(BF16), row(gm_vnorm_g), gm_ws, gm_bs,
               lru_conv_w, row(lru_conv_b), lru_wa.astype(BF16), row(lru_ba), lru_wx.astype(BF16), row(lru_bx),
               row(jax.nn.softplus(-lru_lambda.astype(F32))),
               row(jnp.log(lbs)), row(jnp.log1p(-lbs)), row(1.0 - lbs), row(hg_onorm_g),
               w_branch.astype(BF16), w_out.astype(BF16), final_g.reshape(1, D_MODEL))

    mod = _mod_call(jnp.concatenate([c_prompt, c_sample], axis=0), w_ada, b_ada)
    dt = x_prompt.dtype
    y_p, conv_p, h_p, s_p, _ = _run_group(
        grp_p, x_prompt, mod[:, :batch_p],
        jnp.zeros((DEPTH, batch_p, LRU_CONV - 1, LRU_WIDTH), dt),
        jnp.zeros((DEPTH, batch_p, 1, LRU_WIDTH), dt),
        jnp.zeros((DEPTH, batch_p, HG_HEADS, HG_DK, HG_DV), dt), False, weights)
    y_s, conv_s, h_s, s_s, v_s = _run_group(
        grp_s, x_sample, mod[:, batch_p:],
        state_rglru_conv, state_rglru_h[:, :, None, :], state_hgrn2, True, weights)
    return (y_p, y_s, conv_p, h_p, s_p, conv_s, h_s, s_s, v_s)
```

```python
import functools

import jax
import jax.numpy as jnp
from jax import lax
from jax.experimental import pallas as pl
from jax.experimental.pallas import tpu as pltpu

F32 = jnp.float32
BF16 = jnp.bfloat16

D_MODEL = 2048
DEPTH = 4
GM_CHUNK = 128
GM_GROUPS = 8
GM_WIDTH = 1024
LRU_WIDTH = 1024
LRU_BLOCKS = 8
LRU_CONV = 4
LRU_C = 8.0
HG_CHUNK = 64
HG_HEADS = 8
HG_DK = 128
HG_DV = 128
HG_WIDTH = HG_HEADS * HG_DV
EPS = 1e-6

LANES = 128
SUBLANES = 8
COL = 1024
COL_GM_U, COL_GM_V, COL_GM_G = 0, 1, 2
COL_LRU_X, COL_LRU_G = 3, 4
COL_HG_Q, COL_HG_F, COL_HG_I, COL_HG_O = 5, 6, 7, 8
COL_MERGE = 9
TILE_ROWS = 256
NORM_TILE_ROWS = 1024
HG_MULTISEQ_ROWS = 128
HG_SUB = 16
HG_CLAMP = 60.0
V7X_VMEM_BYTES = 64 * 1024 * 1024


class _Group:
    def __init__(self, batch, seq, tile_rows=TILE_ROWS):
        self.batch, self.seq = batch, seq
        self.ts = min(seq, tile_rows)
        self.nb = min(tile_rows // self.ts, batch)
        assert batch % self.nb == 0 and seq % self.ts == 0
        self.tm = self.nb * self.ts
        self.grid = (batch // self.nb, seq // self.ts)
        self.gm_chunk = min(seq, GM_CHUNK)
        self.hg_chunk = min(seq, HG_CHUNK)
        assert self.ts % self.gm_chunk == 0 and self.ts % self.hg_chunk == 0
        assert self.hg_chunk % HG_SUB == 0
        assert self.nb == 1 or self.grid[1] == 1


def _dot(a, b):
    return jnp.dot(a, b, preferred_element_type=F32)


def _rmsnorm(x, g):
    return x * lax.rsqrt(jnp.mean(x * x, axis=-1, keepdims=True) + EPS) * g


def _adaln(x, g, scale, shift):
    return (_rmsnorm(x, g) * (1.0 + scale) + shift).astype(BF16)


def _tok_spec(grp, width):
    return pl.BlockSpec((grp.nb, grp.ts, width), lambda b, j: (b, j, 0))


def _seq_spec(grp, *tail):
    zeros = (0,) * len(tail)
    return pl.BlockSpec((grp.nb,) + tail, lambda b, j: (b,) + zeros)


def _layer_seq_spec(grp, layer, *tail, **kwargs):
    zeros = (0,) * len(tail)
    return pl.BlockSpec((None, grp.nb) + tail, lambda b, j: (layer, b) + zeros, **kwargs)


def _const_spec(shape):
    nd = len(shape)
    return pl.BlockSpec(shape, lambda b, j: (0,) * nd, pipeline_mode=pl.Buffered(1))


def _layer_spec(layer, *tail):
    zeros = (0,) * len(tail)
    return pl.BlockSpec((None,) + tail, lambda b, j: (layer,) + zeros, pipeline_mode=pl.Buffered(1))


def _win_spec(layer, col):
    return pl.BlockSpec((None, D_MODEL, COL), lambda b, j: (layer, 0, col), pipeline_mode=pl.Buffered(1))


def _wbr_spec(layer, branch):
    return pl.BlockSpec((None, COL, D_MODEL), lambda b, j: (layer, branch, 0), pipeline_mode=pl.Buffered(1))


def _params(vmem_bytes):
    return pltpu.CompilerParams(dimension_semantics=("arbitrary", "arbitrary"),
                                vmem_limit_bytes=min(int(vmem_bytes), V7X_VMEM_BYTES - (4 << 20)))


def _merge_gates(hh, mg0_ref, mg1_ref):
    return [jax.nn.sigmoid(_dot(hh, mg_ref[...])) for mg_ref in (mg0_ref, mg1_ref)]


def _gated_out(gates, y, wbr_ref):
    return jnp.concatenate(
        [g * _dot(y, wbr_ref[:, half * COL:(half + 1) * COL]) for half, g in enumerate(gates)], axis=-1)


def _scan_rows(a, b, seg):
    rows, width = b.shape
    groups = rows // SUBLANES
    b3 = b.reshape(groups, SUBLANES, width)
    a3 = None if a is None else a.reshape(groups, SUBLANES, width)
    pos = lax.broadcasted_iota(jnp.int32, (1, SUBLANES, 1), 1)
    d = 1
    while d < SUBLANES:
        keep = pos >= d
        b_prev = pltpu.roll(b3, d, 1)
        if a3 is None:
            b3 = jnp.where(keep, b3 + b_prev, b3)
        else:
            b3 = jnp.where(keep, a3 * b_prev + b3, b3)
            a3 = jnp.where(keep, a3 * pltpu.roll(a3, d, 1), a3)
        d *= 2
    per_seg = seg // SUBLANES
    out = []
    for g in range(groups):
        hg = b3[g]
        if g % per_seg:
            carry = out[-1][SUBLANES - 1:SUBLANES, :]
            hg = hg + carry if a3 is None else a3[g] * carry + hg
        out.append(hg)
    return jnp.concatenate(out, axis=0)


def _mod_kernel(c_ref, w_ref, b_ref, o_ref):
    sc = jax.nn.silu(c_ref[...]).astype(BF16)
    o_ref[...] = _dot(sc, w_ref[...].astype(BF16)) + b_ref[...]


def _mod_call(c_all, w_ada, b_ada):
    rows = c_all.shape[0]
    width = 3 * D_MODEL
    tn = 1024
    return pl.pallas_call(
        _mod_kernel,
        grid=(DEPTH, width // tn),
        in_specs=[pl.BlockSpec((rows, D_MODEL), lambda l, n: (0, 0)),
                  pl.BlockSpec((None, D_MODEL, tn), lambda l, n: (l, 0, n)),
                  pl.BlockSpec((None, 1, tn), lambda l, n: (l, 0, n))],
        out_specs=pl.BlockSpec((None, rows, tn), lambda l, n: (l, 0, n)),
        out_shape=jax.ShapeDtypeStruct((DEPTH, rows, width), F32),
        compiler_params=_params(2 * D_MODEL * tn * 4 + D_MODEL * tn * 2 + (8 << 20)),
        name="adaln_mod",
    )(c_all, w_ada, b_ada.reshape(DEPTH, 1, width))


def _prenorm_kernel(x_ref, g_ref, scale_ref, shift_ref, h_ref):
    h_ref[...] = _adaln(x_ref[...], g_ref[...], scale_ref[...], shift_ref[...])


def _prenorm_call(grp, layer, x, norm_g, scale, shift):
    return pl.pallas_call(
        _prenorm_kernel,
        grid=grp.grid,
        in_specs=[_tok_spec(grp, D_MODEL), _layer_spec(layer, 1, D_MODEL),
                  _seq_spec(grp, 1, D_MODEL), _seq_spec(grp, 1, D_MODEL)],
        out_specs=_tok_spec(grp, D_MODEL),
        out_shape=jax.ShapeDtypeStruct(x.shape, BF16),
        compiler_params=_params(grp.tm * D_MODEL * (2 * 4 + 2 * 2 + 3 * 4) + (4 << 20)),
        name="prenorm",
    )(x, norm_g, scale, shift)


def _gmlp_kernel(h_ref, wu_ref, wv_ref, wg_ref, mg0_ref, mg1_ref, wbr_ref, vg_ref, wbd_ref, bias_ref,
                 out_ref, *v_out, grp):
    hh = h_ref[...].reshape(grp.tm, D_MODEL)
    v = _rmsnorm(jax.nn.gelu(_dot(hh, wv_ref[...])), vg_ref[...])
    if v_out:
        v_out[0][...] = v.reshape(grp.nb, grp.ts, GM_WIDTH)
    vb = v.astype(BF16)
    width = GM_WIDTH // GM_GROUPS
    s = jnp.concatenate([_dot(wbd_ref[g], vb[:, g * width:(g + 1) * width]) for g in range(GM_GROUPS)],
                        axis=-1) + bias_ref[...]
    u = jax.nn.gelu(_dot(hh, wu_ref[...]))
    ya = (u * s) * jax.nn.silu(_dot(hh, wg_ref[...]))
    oa = _gated_out(_merge_gates(hh, mg0_ref, mg1_ref), ya.astype(BF16), wbr_ref)
    out_ref[...] = oa.astype(BF16).reshape(grp.nb, grp.ts, D_MODEL)


def _gmlp_call(grp, layer, h, w_in, w_branch, vnorm_g, wbd, bias, want_v):
    tm = grp.tm
    out_shape = [jax.ShapeDtypeStruct((grp.batch, grp.seq, D_MODEL), BF16)]
    out_specs = [_tok_spec(grp, D_MODEL)]
    if want_v:
        out_shape.append(jax.ShapeDtypeStruct((grp.batch, grp.seq, GM_WIDTH), F32))
        out_specs.append(_tok_spec(grp, GM_WIDTH))
    resident = (5 * D_MODEL * COL + COL * D_MODEL + GM_GROUPS * tm * tm) * 2 + tm * GM_WIDTH * 4
    streamed = 2 * tm * (D_MODEL * 2 + D_MODEL * 2 + GM_WIDTH * 4)
    temps = 12 * tm * GM_WIDTH * 4
    return pl.pallas_call(
        functools.partial(_gmlp_kernel, grp=grp),
        grid=grp.grid,
        in_specs=[_tok_spec(grp, D_MODEL),
                  _win_spec(layer, COL_GM_U), _win_spec(layer, COL_GM_V), _win_spec(layer, COL_GM_G),
                  _win_spec(layer, COL_MERGE), _win_spec(layer, COL_MERGE + 1), _wbr_spec(layer, 0),
                  _layer_spec(layer, 1, GM_WIDTH), _const_spec(wbd.shape), _const_spec(bias.shape)],
        out_specs=out_specs,
        out_shape=out_shape,
        compiler_params=_params(resident + streamed + temps),
        name="gmlp_branch",
    )(h, w_in, w_in, w_in, w_in, w_in, w_branch, vnorm_g, wbd, bias)


def _hgrn2_kernel(h_ref, wq_ref, wf_ref, wi_ref, wo_ref, mg0_ref, mg1_ref, wbr_ref,
                  loglb_ref, log1mlb_ref, omlb_ref, og_ref, s0_ref,
                  out_ref, sn_ref,
                  qe_scr, qt_scr, kd_scr, kt_scr, v_scr, inter_scr, intra_scr, cq_scr, ck_scr, cv_scr, cg_scr,
                  *, grp):
    nb, tm, L = grp.nb, grp.tm, grp.hg_chunk
    n_chunks = tm // L
    n_sub = L // HG_SUB
    W = HG_WIDTH

    @pl.when(pl.program_id(1) == 0)
    def _():
        sn_ref[...] = s0_ref[...]

    hh = h_ref[...].reshape(tm, D_MODEL)

    def gates(rows_h):
        q = jax.nn.silu(_dot(rows_h, wq_ref[...]))
        zf = _dot(rows_h, wf_ref[...])
        magnitude = jnp.abs(zf)
        e = jnp.exp(-magnitude)
        r = 1.0 / (1.0 + e)
        k = omlb_ref[...] * jnp.where(zf >= 0.0, e * r, r)
        log_sigmoid = jnp.minimum(zf, 0.0) + jnp.log(r)
        a, b = loglb_ref[...], log1mlb_ref[...] + log_sigmoid
        log_f = jnp.maximum(a, b) + jnp.log(1.0 + jnp.exp(-jnp.abs(a - b)))
        return q, k, log_f, _dot(rows_h, wi_ref[...])

    q, k, log_f, v = gates(hh)
    shape4 = (n_chunks, n_sub, HG_SUB, W)
    g_sub = _scan_rows(None, log_f, HG_SUB).reshape(shape4)
    totals = g_sub[:, :, HG_SUB - 1:HG_SUB, :]
    refs = []
    acc = jnp.zeros((n_chunks, 1, 1, W), F32)
    for i in range(n_sub):
        refs.append(acc)
        acc = acc + totals[:, i:i + 1]
    g_last = acc
    G = g_sub + jnp.concatenate(refs, axis=1)
    q4, k4 = q.reshape(shape4), k.reshape(shape4)
    operand = lambda a: a.reshape(tm, W).astype(BF16)
    qe_scr[...] = operand(q4 * jnp.exp(G))
    qt_scr[...] = operand(q4 * jnp.exp(g_sub))
    kd_scr[...] = operand(k4 * jnp.exp(g_last - G))
    for i in range(n_sub):
        kt_scr[i] = operand(k4 * jnp.exp(jnp.minimum(refs[i] - G, HG_CLAMP)))
    v_scr[...] = v.astype(BF16)
    chunk_decay = jnp.exp(g_last).reshape(n_chunks, W)
    out_gate = jax.nn.silu(_dot(hh, wo_ref[...]))
    merge_gates = _merge_gates(hh, mg0_ref, mg1_ref)

    tril = lax.broadcasted_iota(jnp.int32, (L, L), 0) >= lax.broadcasted_iota(jnp.int32, (L, L), 1)
    heads = range(HG_HEADS)
    head_cols = [slice(hd * HG_DK, (hd + 1) * HG_DK) for hd in heads]
    chunk_rows = [slice(c * L, (c + 1) * L) for c in range(n_chunks)]
    contract_last = (((1,), (1,)), ((), ()))
    contract_rows = (((0,), (0,)), ((), ()))

    def score_rows(c, i, cols):
        seen = (i + 1) * HG_SUB
        a = lax.dot_general(qt_scr[c * L + i * HG_SUB:c * L + seen, cols], kt_scr[i, c * L:c * L + seen, cols],
                            contract_last, preferred_element_type=F32)
        return a if seen == L else jnp.concatenate([a, jnp.zeros((HG_SUB, L - seen), F32)], axis=1)

    for c, rows in enumerate(chunk_rows):
        scores = [jnp.concatenate([score_rows(c, i, cols) for i in range(n_sub)], axis=0) for cols in head_cols]
        scores = [jnp.where(tril, A, 0.0).astype(BF16) for A in scores]
        intra_scr[rows, :] = jnp.concatenate([_dot(A, v_scr[rows, cols]) for A, cols in zip(scores, head_cols)], axis=-1)
    updates = [[lax.dot_general(kd_scr[rows, cols], v_scr[rows, cols], contract_rows, preferred_element_type=F32)
                for cols in head_cols] for rows in chunk_rows]
    S = None
    for c, rows in enumerate(chunk_rows):
        slot = 0 if nb == 1 else c
        if nb > 1 or c == 0:
            S = [sn_ref[slot, hd] for hd in heads]
        inter_scr[rows, :] = jnp.concatenate(
            [_dot(qe_scr[rows, cols], S[hd].astype(BF16)) for hd, cols in enumerate(head_cols)], axis=-1)
        S = [jnp.transpose(jnp.broadcast_to(chunk_decay[c:c + 1, cols], (HG_DV, HG_DK))) * S[hd] + updates[c][hd]
             for hd, cols in enumerate(head_cols)]
        if nb > 1 or c == n_chunks - 1:
            for hd in heads:
                sn_ref[slot, hd] = S[hd]

    @pl.when(jnp.min(totals) < -HG_CLAMP)
    def _():
        t_idx = lax.broadcasted_iota(jnp.int32, (L, 1), 0)
        for c in range(n_chunks):
            rows = slice(c * L, (c + 1) * L)
            cq_scr[...], ck_scr[...], log_f_c, cv_scr[...] = gates(hh[rows])
            cg_scr[...] = _scan_rows(None, log_f_c, L)
            for hd in range(HG_HEADS):
                cols = slice(hd * HG_DK, (hd + 1) * HG_DK)
                G_c, q_c = cg_scr[:, cols], cq_scr[:, cols]

                def add_sources(blk, acc):
                    src = pl.ds(pl.multiple_of(blk * SUBLANES, SUBLANES), SUBLANES)
                    g_blk, k_blk, v_blk = cg_scr[src, cols], ck_scr[src, cols], cv_scr[src, cols]
                    for r in range(SUBLANES):
                        g_s, k_s, v_s = g_blk[r:r + 1, :], k_blk[r:r + 1, :], v_blk[r:r + 1, :]
                        w = jnp.sum(q_c * k_s * jnp.exp(jnp.minimum(G_c - g_s, 0.0)), axis=-1, keepdims=True)
                        acc = acc + jnp.where(t_idx >= blk * SUBLANES + r, w, 0.0) * v_s
                    return acc

                intra_scr[rows, cols] = lax.fori_loop(0, L // SUBLANES, add_sources, jnp.zeros((L, HG_DV), F32))

    o = inter_scr[...] + intra_scr[...]
    y = jnp.concatenate(
        [_rmsnorm(o[:, hd * HG_DV:(hd + 1) * HG_DV], og_ref[...]) for hd in range(HG_HEADS)], axis=-1)
    oc = _gated_out(merge_gates, (y * out_gate).astype(BF16), wbr_ref)
    out_ref[...] = oc.astype(BF16).reshape(grp.nb, grp.ts, D_MODEL)


def _hgrn2_call(grp, layer, h, w_in, w_branch, loglb, log1mlb, omlb, onorm_g, s0):
    tm, L = grp.tm, grp.hg_chunk
    n_sub = L // HG_SUB
    state = grp.nb * HG_HEADS * HG_DK * HG_DV * 4
    resident = (6 * D_MODEL * COL + COL * D_MODEL) * 2
    streamed = 2 * tm * (D_MODEL * 2 + D_MODEL * 2) + 3 * state
    scratch = (4 + n_sub) * tm * HG_WIDTH * 2 + 2 * tm * HG_WIDTH * 4 + 4 * L * HG_WIDTH * 4
    temps = 10 * tm * HG_WIDTH * 4
    tile16 = pltpu.VMEM((tm, HG_WIDTH), BF16)
    tile32 = pltpu.VMEM((tm, HG_WIDTH), F32)
    chunk32 = pltpu.VMEM((L, HG_WIDTH), F32)
    return pl.pallas_call(
        functools.partial(_hgrn2_kernel, grp=grp),
        grid=grp.grid,
        in_specs=[_tok_spec(grp, D_MODEL),
                  _win_spec(layer, COL_HG_Q), _win_spec(layer, COL_HG_F), _win_spec(layer, COL_HG_I),
                  _win_spec(layer, COL_HG_O),
                  _win_spec(layer, COL_MERGE + 4), _win_spec(layer, COL_MERGE + 5), _wbr_spec(layer, 2),
                  _layer_spec(layer, 1, HG_WIDTH), _layer_spec(layer, 1, HG_WIDTH), _layer_spec(layer, 1, HG_WIDTH),
                  _layer_spec(layer, 1, HG_DV),
                  _layer_seq_spec(grp, layer, HG_HEADS, HG_DK, HG_DV, pipeline_mode=pl.Buffered(1))],
        out_specs=[_tok_spec(grp, D_MODEL), _seq_spec(grp, HG_HEADS, HG_DK, HG_DV)],
        out_shape=[jax.ShapeDtypeStruct((grp.batch, grp.seq, D_MODEL), BF16),
                   jax.ShapeDtypeStruct((grp.batch, HG_HEADS, HG_DK, HG_DV), F32)],
        scratch_shapes=[tile16, tile16, tile16, pltpu.VMEM((n_sub, tm, HG_WIDTH), BF16), tile16,
                        tile32, tile32, chunk32, chunk32, chunk32, chunk32],
        compiler_params=_params(resident + streamed + scratch + temps),
        name="hgrn2_branch",
    )(h, w_in, w_in, w_in, w_in, w_in, w_in, w_branch, loglb, log1mlb, omlb, onorm_g, s0)


def _rglru_kernel(h_ref, wx_ref, wg_ref, mg0_ref, mg1_ref, wbr_ref, convw_ref, convb_ref, wa_ref, ba_ref,
                  wi_ref, bi_ref, sp_ref, conv0_ref, h0_ref, oa_ref, oc_ref, x_ref, wout_ref, gate_ref, g_ref,
                  *rest, grp, last):
    if last:
        convn_ref, hn_ref, y_ref, tail_scr, hcar = rest
    else:
        scale_ref, shift_ref, convn_ref, hn_ref, xn_ref, hnext_ref, tail_scr, hcar = rest
    nb, ts, tm = grp.nb, grp.ts, grp.tm
    groups = ts // SUBLANES
    history = SUBLANES - (LRU_CONV - 1)

    @pl.when(pl.program_id(1) == 0)
    def _():
        tail_scr[...] = jnp.zeros(tail_scr.shape, F32)
        tail_scr[:, history:, :] = conv0_ref[...]
        hcar[...] = h0_ref[...]

    hh = h_ref[...].reshape(tm, D_MODEL)
    lx = _dot(hh, wx_ref[...]).reshape(nb, groups, SUBLANES, LRU_WIDTH)
    ext = jnp.concatenate([tail_scr[...].reshape(nb, 1, SUBLANES, LRU_WIDTH), lx], axis=1)
    pos = lax.broadcasted_iota(jnp.int32, (1, 1, SUBLANES, 1), 2)
    y = convb_ref[...]
    for tap in range(LRU_CONV):
        shift = LRU_CONV - 1 - tap
        if shift:
            rolled = pltpu.roll(ext.reshape(nb * (groups + 1), SUBLANES, LRU_WIDTH), shift, 1)
            rolled = rolled.reshape(nb, groups + 1, SUBLANES, LRU_WIDTH)
            shifted = jnp.where(pos >= shift, rolled[:, 1:], rolled[:, :groups])
        else:
            shifted = lx
        y = y + convw_ref[tap:tap + 1, :] * shifted
    tail = lx[:, groups - 1]
    tail_scr[...] = tail
    convn_ref[...] = tail[:, history:, :]

    x = y.reshape(tm, LRU_WIDTH)
    xb = x.astype(BF16)
    width = LRU_WIDTH // LRU_BLOCKS

    def block_diag(w_ref, b_ref):
        return jnp.concatenate([_dot(xb[:, n * width:(n + 1) * width], w_ref[n]) for n in range(LRU_BLOCKS)],
                               axis=-1) + b_ref[...]

    gate_r = jax.nn.sigmoid(block_diag(wa_ref, ba_ref))
    gate_i = jax.nn.sigmoid(block_diag(wi_ref, bi_ref))
    log_a = -LRU_C * gate_r * sp_ref[...]
    a = jnp.exp(log_a)
    drive = jnp.sqrt(-jnp.tanh(log_a) * (a * a + 1.0)) * gate_i * x
    h0 = jnp.broadcast_to(hcar[...], (nb, ts, LRU_WIDTH)).reshape(tm, LRU_WIDTH)
    first = lax.broadcasted_iota(jnp.int32, (tm, 1), 0) % ts == 0
    drive = drive + jnp.where(first, a * h0, 0.0)
    hseq = _scan_rows(a, drive, ts)
    h_last = hseq.reshape(nb, ts, LRU_WIDTH)[:, ts - 1:ts, :]
    hcar[...] = h_last
    hn_ref[...] = h_last
    yb = hseq * jax.nn.silu(_dot(hh, wg_ref[...]))
    ob = _gated_out(_merge_gates(hh, mg0_ref, mg1_ref), yb.astype(BF16), wbr_ref)

    as_rows = lambda ref: ref[...].reshape(tm, D_MODEL).astype(F32)
    merged = (as_rows(oa_ref) + ob) + as_rows(oc_ref)
    out = _dot(merged.astype(BF16), wout_ref[...]).reshape(nb, ts, D_MODEL)
    xn = x_ref[...] + gate_ref[...] * out
    if last:
        y_ref[...] = _rmsnorm(xn, g_ref[...])
    else:
        xn_ref[...] = xn
        hnext_ref[...] = _adaln(xn, g_ref[...], scale_ref[...], shift_ref[...])


def _rglru_call(grp, layer, h, w_in, w_branch, w_out, conv_w, conv_b, wa, ba, wi, bi, sp, conv0, h0,
                oa, oc, x, gate, norm_g, scale=None, shift=None):
    last = scale is None
    tm = grp.tm
    tok = _tok_spec(grp, D_MODEL)
    in_specs = [tok,
                _win_spec(layer, COL_LRU_X), _win_spec(layer, COL_LRU_G),
                _win_spec(layer, COL_MERGE + 2), _win_spec(layer, COL_MERGE + 3), _wbr_spec(layer, 1),
                _layer_spec(layer, LRU_CONV, LRU_WIDTH), _layer_spec(layer, 1, LRU_WIDTH),
                _layer_spec(layer, LRU_BLOCKS, LANES, LANES), _layer_spec(layer, 1, LRU_WIDTH),
                _layer_spec(layer, LRU_BLOCKS, LANES, LANES), _layer_spec(layer, 1, LRU_WIDTH),
                _layer_spec(layer, 1, LRU_WIDTH),
                _layer_seq_spec(grp, layer, LRU_CONV - 1, LRU_WIDTH), _layer_seq_spec(grp, layer, 1, LRU_WIDTH),
                tok, tok, tok, _layer_spec(layer, D_MODEL, D_MODEL), _seq_spec(grp, 1, D_MODEL)]
    args = [h, w_in, w_in, w_in, w_in, w_branch, conv_w, conv_b, wa, ba, wi, bi, sp, conv0, h0,
            oa, oc, x, w_out, gate]
    out_specs = [_seq_spec(grp, LRU_CONV - 1, LRU_WIDTH), _seq_spec(grp, 1, LRU_WIDTH)]
    out_shape = [jax.ShapeDtypeStruct((grp.batch, LRU_CONV - 1, LRU_WIDTH), F32),
                 jax.ShapeDtypeStruct((grp.batch, 1, LRU_WIDTH), F32)]
    if last:
        in_specs.append(pl.BlockSpec((1, D_MODEL), lambda b, j: (0, 0)))
        args.append(norm_g)
        out_specs.append(tok)
        out_shape.append(jax.ShapeDtypeStruct(x.shape, F32))
    else:
        in_specs += [_layer_spec(layer + 1, 1, D_MODEL), _seq_spec(grp, 1, D_MODEL), _seq_spec(grp, 1, D_MODEL)]
        args += [norm_g, scale, shift]
        out_specs += [tok, tok]
        out_shape += [jax.ShapeDtypeStruct(x.shape, F32), jax.ShapeDtypeStruct(x.shape, BF16)]
    resident = (4 * D_MODEL * COL + COL * D_MODEL + D_MODEL * D_MODEL + 2 * LRU_WIDTH * LRU_WIDTH // LRU_BLOCKS) * 2
    streamed = 2 * tm * D_MODEL * (3 * 2 + 2 * 4 + 2)
    scratch = grp.nb * SUBLANES * LRU_WIDTH * 4
    temps = 8 * tm * D_MODEL * 4
    return pl.pallas_call(
        functools.partial(_rglru_kernel, grp=grp, last=last),
        grid=grp.grid,
        in_specs=in_specs,
        out_specs=out_specs,
        out_shape=out_shape,
        scratch_shapes=[pltpu.VMEM((grp.nb, SUBLANES, LRU_WIDTH), F32),
                        pltpu.VMEM((grp.nb, 1, LRU_WIDTH), F32)],
        compiler_params=_params(resident + streamed + scratch + temps),
        name="rglru_merge",
    )(*args)


def _gmlp_mix_weights(grp, ws, bs):
    L = grp.gm_chunk
    reps = grp.tm // L
    w = jnp.where(jnp.tril(jnp.ones((L, L), dtype=bool))[None], ws[:, :L, :L], 0.0)
    wbd = jnp.einsum('ab,gts->gatbs', jnp.eye(reps, dtype=F32), w).reshape(GM_GROUPS, grp.tm, grp.tm)
    bias = jnp.repeat(jnp.tile(bs[:, :L].T, (reps, 1)), GM_WIDTH // GM_GROUPS, axis=1)
    return wbd.astype(BF16), bias


def _run_group(grp, x, mod, conv0, h0, s0, want_v, weights):
    (norm_g, w_in, gm_vnorm_g, gm_ws, gm_bs, conv_w, conv_b, wa, ba, wi, bi, sp,
     loglb, log1mlb, omlb, onorm_g, w_branch, w_out, final_g) = weights
    shift = mod[:, :, None, 0:D_MODEL]
    scale = mod[:, :, None, D_MODEL:2 * D_MODEL]
    gate = mod[:, :, None, 2 * D_MODEL:]
    h = _prenorm_call(_Group(grp.batch, grp.seq, NORM_TILE_ROWS), 0, x, norm_g, scale[0], shift[0])
    hg_grp = grp if grp.nb == 1 else _Group(grp.batch, grp.seq, HG_MULTISEQ_ROWS)
    convs, hs, ss, vs = [], [], [], []
    y = None
    for l in range(DEPTH):
        wbd, bias = _gmlp_mix_weights(grp, gm_ws[l], gm_bs[l])
        res = _gmlp_call(grp, l, h, w_in, w_branch, gm_vnorm_g, wbd, bias, want_v)
        oa = res[0]
        if want_v:
            vs.append(res[1])
        oc, s_n = _hgrn2_call(hg_grp, l, h, w_in, w_branch, loglb, log1mlb, omlb, onorm_g, s0)
        lru = (grp, l, h, w_in, w_branch, w_out, conv_w, conv_b, wa, ba, wi, bi, sp, conv0, h0, oa, oc, x, gate[l])
        if l + 1 < DEPTH:
            conv_n, h_n, x, h = _rglru_call(*lru, norm_g, scale[l + 1], shift[l + 1])
        else:
            conv_n, h_n, y = _rglru_call(*lru, final_g)
        convs.append(conv_n)
        hs.append(h_n[:, 0, :])
        ss.append(s_n)
    return y, jnp.stack(convs), jnp.stack(hs), jnp.stack(ss), (jnp.stack(vs) if want_v else None)


def kernel(x_prompt, x_sample, c_prompt, c_sample, state_rglru_conv, state_rglru_h, state_hgrn2,
           w_ada, b_ada, norm_g, w_in, gm_vnorm_g, gm_ws, gm_bs, lru_conv_w, lru_conv_b,
           lru_wa, lru_ba, lru_wx, lru_bx, lru_lambda, hg_lb, hg_onorm_g, w_branch, w_out, final_g):
    batch_p, seq_p, _ = x_prompt.shape
    batch_s, seq_s, _ = x_sample.shape
    grp_p, grp_s = _Group(batch_p, seq_p), _Group(batch_s, seq_s)

    p = jax.nn.softmax(hg_lb.astype(F32), axis=0)
    cs = jnp.cumsum(p, axis=0)
    lbs = cs - cs[0]
    row = lambda a: a.reshape(DEPTH, 1, a.shape[-1])
    weights = (row(norm_g), w_in.astype(BF16), row(gm_vnorm_g), gm_ws, gm_bs,
               lru_conv_w, row(lru_conv_b), lru_wa.astype(BF16), row(lru_ba), lru_wx.astype(BF16), row(lru_bx),
               row(jax.nn.softplus(-lru_lambda.astype(F32))),
               row(jnp.log(lbs)), row(jnp.log1p(-lbs)), row(1.0 - lbs), row(hg_onorm_g),
               w_branch.astype(BF16), w_out.astype(BF16), final_g.reshape(1, D_MODEL))

    mod = _mod_call(jnp.concatenate([c_prompt, c_sample], axis=0), w_ada, b_ada)
    dt = x_prompt.dtype
    y_p, conv_p, h_p, s_p, _ = _run_group(
        grp_p, x_prompt, mod[:, :batch_p],
        jnp.zeros((DEPTH, batch_p, LRU_CONV - 1, LRU_WIDTH), dt),
        jnp.zeros((DEPTH, batch_p, 1, LRU_WIDTH), dt),
        jnp.zeros((DEPTH, batch_p, HG_HEADS, HG_DK, HG_DV), dt), False, weights)
    y_s, conv_s, h_s, s_s, v_s = _run_group(
        grp_s, x_sample, mod[:, batch_p:],
        state_rglru_conv, state_rglru_h[:, :, None, :], state_hgrn2, True, weights)
    return (y_p, y_s, conv_p, h_p, s_p, conv_s, h_s, s_s, v_s)
```

```python
import functools

import jax
import jax.numpy as jnp
from jax import lax
from jax.experimental import pallas as pl
from jax.experimental.pallas import tpu as pltpu

F32 = jnp.float32
BF16 = jnp.bfloat16

D_MODEL = 2048
DEPTH = 4
GM_CHUNK = 128
GM_GROUPS = 8
GM_WIDTH = 1024
LRU_WIDTH = 1024
LRU_BLOCKS = 8
LRU_CONV = 4
LRU_C = 8.0
HG_CHUNK = 64
HG_HEADS = 8
HG_DK = 128
HG_DV = 128
HG_WIDTH = HG_HEADS * HG_DV
EPS = 1e-6

LANES = 128
SUBLANES = 8
COL = 1024
COL_GM_U, COL_GM_V, COL_GM_G = 0, 1, 2
COL_LRU_X, COL_LRU_G = 3, 4
COL_HG_Q, COL_HG_F, COL_HG_I, COL_HG_O = 5, 6, 7, 8
COL_MERGE = 9
TILE_ROWS = 256
NORM_TILE_ROWS = 1024
HG_MULTISEQ_ROWS = 128
HG_SUB = 16
HG_CLAMP = 60.0
V7X_VMEM_BYTES = 64 * 1024 * 1024


class _Group:
    def __init__(self, batch, seq, tile_rows=TILE_ROWS):
        self.batch, self.seq = batch, seq
        self.ts = min(seq, tile_rows)
        self.nb = min(tile_rows // self.ts, batch)
        assert batch % self.nb == 0 and seq % self.ts == 0
        self.tm = self.nb * self.ts
        self.grid = (batch // self.nb, seq // self.ts)
        self.gm_chunk = min(seq, GM_CHUNK)
        self.hg_chunk = min(seq, HG_CHUNK)
        assert self.ts % self.gm_chunk == 0 and self.ts % self.hg_chunk == 0
        assert self.hg_chunk % HG_SUB == 0
        assert self.nb == 1 or self.grid[1] == 1


def _dot(a, b):
    return jnp.dot(a, b, preferred_element_type=F32)


def _rmsnorm(x, g):
    return x * lax.rsqrt(jnp.mean(x * x, axis=-1, keepdims=True) + EPS) * g


def _adaln(x, g, scale, shift):
    return (_rmsnorm(x, g) * (1.0 + scale) + shift).astype(BF16)


def _tok_spec(grp, width):
    return pl.BlockSpec((grp.nb, grp.ts, width), lambda b, j: (b, j, 0))


def _seq_spec(grp, *tail):
    zeros = (0,) * len(tail)
    return pl.BlockSpec((grp.nb,) + tail, lambda b, j: (b,) + zeros)


def _layer_seq_spec(grp, layer, *tail, **kwargs):
    zeros = (0,) * len(tail)
    return pl.BlockSpec((None, grp.nb) + tail, lambda b, j: (layer, b) + zeros, **kwargs)


def _const_spec(shape):
    nd = len(shape)
    return pl.BlockSpec(shape, lambda b, j: (0,) * nd, pipeline_mode=pl.Buffered(1))


def _layer_spec(layer, *tail):
    zeros = (0,) * len(tail)
    return pl.BlockSpec((None,) + tail, lambda b, j: (layer,) + zeros, pipeline_mode=pl.Buffered(1))


def _win_spec(layer, col):
    return pl.BlockSpec((None, D_MODEL, COL), lambda b, j: (layer, 0, col), pipeline_mode=pl.Buffered(1))


def _wbr_spec(layer, branch):
    return pl.BlockSpec((None, COL, D_MODEL), lambda b, j: (layer, branch, 0), pipeline_mode=pl.Buffered(1))


def _params(vmem_bytes):
    return pltpu.CompilerParams(dimension_semantics=("arbitrary", "arbitrary"),
                                vmem_limit_bytes=min(int(vmem_bytes), V7X_VMEM_BYTES - (4 << 20)))


def _merge_gates(hh, mg0_ref, mg1_ref):
    return [jax.nn.sigmoid(_dot(hh, mg_ref[...])) for mg_ref in (mg0_ref, mg1_ref)]


def _gated_out(gates, y, wbr_ref):
    return jnp.concatenate(
        [g * _dot(y, wbr_ref[:, half * COL:(half + 1) * COL]) for half, g in enumerate(gates)], axis=-1)


def _scan_rows(a, b, seg):
    rows, width = b.shape
    groups = rows // SUBLANES
    b3 = b.reshape(groups, SUBLANES, width)
    a3 = None if a is None else a.reshape(groups, SUBLANES, width)
    pos = lax.broadcasted_iota(jnp.int32, (1, SUBLANES, 1), 1)
    d = 1
    while d < SUBLANES:
        keep = pos >= d
        b_prev = pltpu.roll(b3, d, 1)
        if a3 is None:
            b3 = jnp.where(keep, b3 + b_prev, b3)
        else:
            b3 = jnp.where(keep, a3 * b_prev + b3, b3)
            a3 = jnp.where(keep, a3 * pltpu.roll(a3, d, 1), a3)
        d *= 2
    per_seg = seg // SUBLANES
    out = []
    for g in range(groups):
        hg = b3[g]
        if g % per_seg:
            carry = out[-1][SUBLANES - 1:SUBLANES, :]
            hg = hg + carry if a3 is None else a3[g] * carry + hg
        out.append(hg)
    return jnp.concatenate(out, axis=0)


def _mod_kernel(c_ref, w_ref, b_ref, o_ref):
    sc = jax.nn.silu(c_ref[...]).astype(BF16)
    o_ref[...] = _dot(sc, w_ref[...].astype(BF16)) + b_ref[...]


def _mod_call(c_all, w_ada, b_ada):
    rows = c_all.shape[0]
    width = 3 * D_MODEL
    tn = 1024
    return pl.pallas_call(
        _mod_kernel,
        grid=(DEPTH, width // tn),
        in_specs=[pl.BlockSpec((rows, D_MODEL), lambda l, n: (0, 0)),
                  pl.BlockSpec((None, D_MODEL, tn), lambda l, n: (l, 0, n)),
                  pl.BlockSpec((None, 1, tn), lambda l, n: (l, 0, n))],
        out_specs=pl.BlockSpec((None, rows, tn), lambda l, n: (l, 0, n)),
        out_shape=jax.ShapeDtypeStruct((DEPTH, rows, width), F32),
        compiler_params=_params(2 * D_MODEL * tn * 4 + D_MODEL * tn * 2 + (8 << 20)),
        name="adaln_mod",
    )(c_all, w_ada, b_ada.reshape(DEPTH, 1, width))


def _prenorm_kernel(x_ref, g_ref, scale_ref, shift_ref, h_ref):
    h_ref[...] = _adaln(x_ref[...], g_ref[...], scale_ref[...], shift_ref[...])


def _prenorm_call(grp, layer, x, norm_g, scale, shift):
    return pl.pallas_call(
        _prenorm_kernel,
        grid=grp.grid,
        in_specs=[_tok_spec(grp, D_MODEL), _layer_spec(layer, 1, D_MODEL),
                  _seq_spec(grp, 1, D_MODEL), _seq_spec(grp, 1, D_MODEL)],
        out_specs=_tok_spec(grp, D_MODEL),
        out_shape=jax.ShapeDtypeStruct(x.shape, BF16),
        compiler_params=_params(grp.tm * D_MODEL * (2 * 4 + 2 * 2 + 3 * 4) + (4 << 20)),
        name="prenorm",
    )(x, norm_g, scale, shift)


def _gmlp_kernel(h_ref, wu_ref, wv_ref, wg_ref, mg0_ref, mg1_ref, wbr_ref, vg_ref, wbd_ref, bias_ref,
                 out_ref, *v_out, grp):
    hh = h_ref[...].reshape(grp.tm, D_MODEL)
    zv = _dot(hh, wv_ref[...])
    zu = _dot(hh, wu_ref[...])
    zg = _dot(hh, wg_ref[...])
    merge_gates = _merge_gates(hh, mg0_ref, mg1_ref)
    v = _rmsnorm(jax.nn.gelu(zv), vg_ref[...])
    if v_out:
        v_out[0][...] = v.reshape(grp.nb, grp.ts, GM_WIDTH)
    vb = v.astype(BF16)
    width = GM_WIDTH // GM_GROUPS
    s = jnp.concatenate([_dot(wbd_ref[g], vb[:, g * width:(g + 1) * width]) for g in range(GM_GROUPS)],
                        axis=-1) + bias_ref[...]
    ya = (jax.nn.gelu(zu) * s) * jax.nn.silu(zg)
    oa = _gated_out(merge_gates, ya.astype(BF16), wbr_ref)
    out_ref[...] = oa.astype(BF16).reshape(grp.nb, grp.ts, D_MODEL)


def _gmlp_call(grp, layer, h, w_in, w_branch, vnorm_g, wbd, bias, want_v):
    tm = grp.tm
    out_shape = [jax.ShapeDtypeStruct((grp.batch, grp.seq, D_MODEL), BF16)]
    out_specs = [_tok_spec(grp, D_MODEL)]
    if want_v:
        out_shape.append(jax.ShapeDtypeStruct((grp.batch, grp.seq, GM_WIDTH), F32))
        out_specs.append(_tok_spec(grp, GM_WIDTH))
    resident = (5 * D_MODEL * COL + COL * D_MODEL + GM_GROUPS * tm * tm) * 2 + tm * GM_WIDTH * 4
    streamed = 2 * tm * (D_MODEL * 2 + D_MODEL * 2 + GM_WIDTH * 4)
    temps = 12 * tm * GM_WIDTH * 4
    return pl.pallas_call(
        functools.partial(_gmlp_kernel, grp=grp),
        grid=grp.grid,
        in_specs=[_tok_spec(grp, D_MODEL),
                  _win_spec(layer, COL_GM_U), _win_spec(layer, COL_GM_V), _win_spec(layer, COL_GM_G),
                  _win_spec(layer, COL_MERGE), _win_spec(layer, COL_MERGE + 1), _wbr_spec(layer, 0),
                  _layer_spec(layer, 1, GM_WIDTH), _const_spec(wbd.shape), _const_spec(bias.shape)],
        out_specs=out_specs,
        out_shape=out_shape,
        compiler_params=_params(resident + streamed + temps),
        name="gmlp_branch",
    )(h, w_in, w_in, w_in, w_in, w_in, w_branch, vnorm_g, wbd, bias)


def _hgrn2_kernel(h_ref, wq_ref, wf_ref, wi_ref, wo_ref, mg0_ref, mg1_ref, wbr_ref,
                  loglb_ref, log1mlb_ref, omlb_ref, og_ref, s0_ref,
                  out_ref, sn_ref,
                  qe_scr, qt_scr, kd_scr, kt_scr, v_scr, inter_scr, intra_scr, cq_scr, ck_scr, cv_scr, cg_scr,
                  *, grp):
    nb, tm, L = grp.nb, grp.tm, grp.hg_chunk
    n_chunks = tm // L
    n_sub = L // HG_SUB
    W = HG_WIDTH

    @pl.when(pl.program_id(1) == 0)
    def _():
        sn_ref[...] = s0_ref[...]

    hh = h_ref[...].reshape(tm, D_MODEL)

    def gates(rows_h):
        q = jax.nn.silu(_dot(rows_h, wq_ref[...]))
        zf = _dot(rows_h, wf_ref[...])
        magnitude = jnp.abs(zf)
        e = jnp.exp(-magnitude)
        r = 1.0 / (1.0 + e)
        k = omlb_ref[...] * jnp.where(zf >= 0.0, e * r, r)
        log_sigmoid = jnp.minimum(zf, 0.0) + jnp.log(r)
        a, b = loglb_ref[...], log1mlb_ref[...] + log_sigmoid
        log_f = jnp.maximum(a, b) + jnp.log(1.0 + jnp.exp(-jnp.abs(a - b)))
        return q, k, log_f, _dot(rows_h, wi_ref[...])

    q, k, log_f, v = gates(hh)
    shape4 = (n_chunks, n_sub, HG_SUB, W)
    g_sub = _scan_rows(None, log_f, HG_SUB).reshape(shape4)
    totals = g_sub[:, :, HG_SUB - 1:HG_SUB, :]
    refs = []
    acc = jnp.zeros((n_chunks, 1, 1, W), F32)
    for i in range(n_sub):
        refs.append(acc)
        acc = acc + totals[:, i:i + 1]
    g_last = acc
    G = g_sub + jnp.concatenate(refs, axis=1)
    q4, k4 = q.reshape(shape4), k.reshape(shape4)
    operand = lambda a: a.reshape(tm, W).astype(BF16)
    qe_scr[...] = operand(q4 * jnp.exp(G))
    qt_scr[...] = operand(q4 * jnp.exp(g_sub))
    kd_scr[...] = operand(k4 * jnp.exp(g_last - G))
    for i in range(n_sub):
        kt_scr[i] = operand(k4 * jnp.exp(jnp.minimum(refs[i] - G, HG_CLAMP)))
    v_scr[...] = v.astype(BF16)
    chunk_decay = jnp.exp(g_last).reshape(n_chunks, W)
    out_gate = jax.nn.silu(_dot(hh, wo_ref[...]))
    merge_gates = _merge_gates(hh, mg0_ref, mg1_ref)

    tril = lax.broadcasted_iota(jnp.int32, (L, L), 0) >= lax.broadcasted_iota(jnp.int32, (L, L), 1)
    heads = range(HG_HEADS)
    head_cols = [slice(hd * HG_DK, (hd + 1) * HG_DK) for hd in heads]
    chunk_rows = [slice(c * L, (c + 1) * L) for c in range(n_chunks)]
    contract_last = (((1,), (1,)), ((), ()))
    contract_rows = (((0,), (0,)), ((), ()))

    def score_rows(c, i, cols):
        seen = (i + 1) * HG_SUB
        a = lax.dot_general(qt_scr[c * L + i * HG_SUB:c * L + seen, cols], kt_scr[i, c * L:c * L + seen, cols],
                            contract_last, preferred_element_type=F32)
        return a if seen == L else jnp.concatenate([a, jnp.zeros((HG_SUB, L - seen), F32)], axis=1)

    for c, rows in enumerate(chunk_rows):
        scores = [jnp.concatenate([score_rows(c, i, cols) for i in range(n_sub)], axis=0) for cols in head_cols]
        scores = [jnp.where(tril, A, 0.0).astype(BF16) for A in scores]
        intra_scr[rows, :] = jnp.concatenate([_dot(A, v_scr[rows, cols]) for A, cols in zip(scores, head_cols)], axis=-1)
    updates = [[lax.dot_general(kd_scr[rows, cols], v_scr[rows, cols], contract_rows, preferred_element_type=F32)
                for cols in head_cols] for rows in chunk_rows]
    S = None
    for c, rows in enumerate(chunk_rows):
        slot = 0 if nb == 1 else c
        if nb > 1 or c == 0:
            S = [sn_ref[slot, hd] for hd in heads]
        inter_scr[rows, :] = jnp.concatenate(
            [_dot(qe_scr[rows, cols], S[hd].astype(BF16)) for hd, cols in enumerate(head_cols)], axis=-1)
        S = [jnp.transpose(jnp.broadcast_to(chunk_decay[c:c + 1, cols], (HG_DV, HG_DK))) * S[hd] + updates[c][hd]
             for hd, cols in enumerate(head_cols)]
        if nb > 1 or c == n_chunks - 1:
            for hd in heads:
                sn_ref[slot, hd] = S[hd]

    @pl.when(jnp.min(totals) < -HG_CLAMP)
    def _():
        t_idx = lax.broadcasted_iota(jnp.int32, (L, 1), 0)
        for c in range(n_chunks):
            rows = slice(c * L, (c + 1) * L)
            cq_scr[...], ck_scr[...], log_f_c, cv_scr[...] = gates(hh[rows])
            cg_scr[...] = _scan_rows(None, log_f_c, L)
            for hd in range(HG_HEADS):
                cols = slice(hd * HG_DK, (hd + 1) * HG_DK)
                G_c, q_c = cg_scr[:, cols], cq_scr[:, cols]

                def add_sources(blk, acc):
                    src = pl.ds(pl.multiple_of(blk * SUBLANES, SUBLANES), SUBLANES)
                    g_blk, k_blk, v_blk = cg_scr[src, cols], ck_scr[src, cols], cv_scr[src, cols]
                    for r in range(SUBLANES):
                        g_s, k_s, v_s = g_blk[r:r + 1, :], k_blk[r:r + 1, :], v_blk[r:r + 1, :]
                        w = jnp.sum(q_c * k_s * jnp.exp(jnp.minimum(G_c - g_s, 0.0)), axis=-1, keepdims=True)
                        acc = acc + jnp.where(t_idx >= blk * SUBLANES + r, w, 0.0) * v_s
                    return acc

                intra_scr[rows, cols] = lax.fori_loop(0, L // SUBLANES, add_sources, jnp.zeros((L, HG_DV), F32))

    o = inter_scr[...] + intra_scr[...]
    y = jnp.concatenate(
        [_rmsnorm(o[:, hd * HG_DV:(hd + 1) * HG_DV], og_ref[...]) for hd in range(HG_HEADS)], axis=-1)
    oc = _gated_out(merge_gates, (y * out_gate).astype(BF16), wbr_ref)
    out_ref[...] = oc.astype(BF16).reshape(grp.nb, grp.ts, D_MODEL)


def _hgrn2_call(grp, layer, h, w_in, w_branch, loglb, log1mlb, omlb, onorm_g, s0):
    tm, L = grp.tm, grp.hg_chunk
    n_sub = L // HG_SUB
    state = grp.nb * HG_HEADS * HG_DK * HG_DV * 4
    resident = (6 * D_MODEL * COL + COL * D_MODEL) * 2
    streamed = 2 * tm * (D_MODEL * 2 + D_MODEL * 2) + 3 * state
    scratch = (4 + n_sub) * tm * HG_WIDTH * 2 + 2 * tm * HG_WIDTH * 4 + 4 * L * HG_WIDTH * 4
    temps = 10 * tm * HG_WIDTH * 4
    tile16 = pltpu.VMEM((tm, HG_WIDTH), BF16)
    tile32 = pltpu.VMEM((tm, HG_WIDTH), F32)
    chunk32 = pltpu.VMEM((L, HG_WIDTH), F32)
    return pl.pallas_call(
        functools.partial(_hgrn2_kernel, grp=grp),
        grid=grp.grid,
        in_specs=[_tok_spec(grp, D_MODEL),
                  _win_spec(layer, COL_HG_Q), _win_spec(layer, COL_HG_F), _win_spec(layer, COL_HG_I),
                  _win_spec(layer, COL_HG_O),
                  _win_spec(layer, COL_MERGE + 4), _win_spec(layer, COL_MERGE + 5), _wbr_spec(layer, 2),
                  _layer_spec(layer, 1, HG_WIDTH), _layer_spec(layer, 1, HG_WIDTH), _layer_spec(layer, 1, HG_WIDTH),
                  _layer_spec(layer, 1, HG_DV),
                  _layer_seq_spec(grp, layer, HG_HEADS, HG_DK, HG_DV, pipeline_mode=pl.Buffered(1))],
        out_specs=[_tok_spec(grp, D_MODEL), _seq_spec(grp, HG_HEADS, HG_DK, HG_DV)],
        out_shape=[jax.ShapeDtypeStruct((grp.batch, grp.seq, D_MODEL), BF16),
                   jax.ShapeDtypeStruct((grp.batch, HG_HEADS, HG_DK, HG_DV), F32)],
        scratch_shapes=[tile16, tile16, tile16, pltpu.VMEM((n_sub, tm, HG_WIDTH), BF16), tile16,
                        tile32, tile32, chunk32, chunk32, chunk32, chunk32],
        compiler_params=_params(resident + streamed + scratch + temps),
        name="hgrn2_branch",
    )(h, w_in, w_in, w_in, w_in, w_in, w_in, w_branch, loglb, log1mlb, omlb, onorm_g, s0)


def _rglru_kernel(h_ref, wx_ref, wg_ref, mg0_ref, mg1_ref, wbr_ref, convw_ref, convb_ref, wa_ref, ba_ref,
                  wi_ref, bi_ref, sp_ref, conv0_ref, h0_ref, oa_ref, oc_ref, x_ref, wout_ref, gate_ref, g_ref,
                  *rest, grp, last):
    if last:
        convn_ref, hn_ref, y_ref, tail_scr, hcar = rest
    else:
        scale_ref, shift_ref, convn_ref, hn_ref, xn_ref, hnext_ref, tail_scr, hcar = rest
    nb, ts, tm = grp.nb, grp.ts, grp.tm
    groups = ts // SUBLANES
    history = SUBLANES - (LRU_CONV - 1)

    @pl.when(pl.program_id(1) == 0)
    def _():
        tail_scr[...] = jnp.zeros(tail_scr.shape, F32)
        tail_scr[:, history:, :] = conv0_ref[...]
        hcar[...] = h0_ref[...]

    hh = h_ref[...].reshape(tm, D_MODEL)
    lx = _dot(hh, wx_ref[...]).reshape(nb, groups, SUBLANES, LRU_WIDTH)
    ext = jnp.concatenate([tail_scr[...].reshape(nb, 1, SUBLANES, LRU_WIDTH), lx], axis=1)
    pos = lax.broadcasted_iota(jnp.int32, (1, 1, SUBLANES, 1), 2)
    y = convb_ref[...]
    for tap in range(LRU_CONV):
        shift = LRU_CONV - 1 - tap
        if shift:
            rolled = pltpu.roll(ext.reshape(nb * (groups + 1), SUBLANES, LRU_WIDTH), shift, 1)
            rolled = rolled.reshape(nb, groups + 1, SUBLANES, LRU_WIDTH)
            shifted = jnp.where(pos >= shift, rolled[:, 1:], rolled[:, :groups])
        else:
            shifted = lx
        y = y + convw_ref[tap:tap + 1, :] * shifted
    tail = lx[:, groups - 1]
    tail_scr[...] = tail
    convn_ref[...] = tail[:, history:, :]

    x = y.reshape(tm, LRU_WIDTH)
    xb = x.astype(BF16)
    width = LRU_WIDTH // LRU_BLOCKS

    def block_diag(w_ref, b_ref):
        return jnp.concatenate([_dot(xb[:, n * width:(n + 1) * width], w_ref[n]) for n in range(LRU_BLOCKS)],
                               axis=-1) + b_ref[...]

    gate_r = jax.nn.sigmoid(block_diag(wa_ref, ba_ref))
    gate_i = jax.nn.sigmoid(block_diag(wi_ref, bi_ref))
    log_a = -LRU_C * gate_r * sp_ref[...]
    a = jnp.exp(log_a)
    drive = jnp.sqrt(-jnp.tanh(log_a) * (a * a + 1.0)) * gate_i * x
    h0 = jnp.broadcast_to(hcar[...], (nb, ts, LRU_WIDTH)).reshape(tm, LRU_WIDTH)
    first = lax.broadcasted_iota(jnp.int32, (tm, 1), 0) % ts == 0
    drive = drive + jnp.where(first, a * h0, 0.0)
    hseq = _scan_rows(a, drive, ts)
    h_last = hseq.reshape(nb, ts, LRU_WIDTH)[:, ts - 1:ts, :]
    hcar[...] = h_last
    hn_ref[...] = h_last
    yb = hseq * jax.nn.silu(_dot(hh, wg_ref[...]))
    ob = _gated_out(_merge_gates(hh, mg0_ref, mg1_ref), yb.astype(BF16), wbr_ref)

    as_rows = lambda ref: ref[...].reshape(tm, D_MODEL).astype(F32)
    merged = (as_rows(oa_ref) + ob) + as_rows(oc_ref)
    out = _dot(merged.astype(BF16), wout_ref[...]).reshape(nb, ts, D_MODEL)
    xn = x_ref[...] + gate_ref[...] * out
    if last:
        y_ref[...] = _rmsnorm(xn, g_ref[...])
    else:
        xn_ref[...] = xn
        hnext_ref[...] = _adaln(xn, g_ref[...], scale_ref[...], shift_ref[...])


def _rglru_call(grp, layer, h, w_in, w_branch, w_out, conv_w, conv_b, wa, ba, wi, bi, sp, conv0, h0,
                oa, oc, x, gate, norm_g, scale=None, shift=None):
    last = scale is None
    tm = grp.tm
    tok = _tok_spec(grp, D_MODEL)
    in_specs = [tok,
                _win_spec(layer, COL_LRU_X), _win_spec(layer, COL_LRU_G),
                _win_spec(layer, COL_MERGE + 2), _win_spec(layer, COL_MERGE + 3), _wbr_spec(layer, 1),
                _layer_spec(layer, LRU_CONV, LRU_WIDTH), _layer_spec(layer, 1, LRU_WIDTH),
                _layer_spec(layer, LRU_BLOCKS, LANES, LANES), _layer_spec(layer, 1, LRU_WIDTH),
                _layer_spec(layer, LRU_BLOCKS, LANES, LANES), _layer_spec(layer, 1, LRU_WIDTH),
                _layer_spec(layer, 1, LRU_WIDTH),
                _layer_seq_spec(grp, layer, LRU_CONV - 1, LRU_WIDTH), _layer_seq_spec(grp, layer, 1, LRU_WIDTH),
                tok, tok, tok, _layer_spec(layer, D_MODEL, D_MODEL), _seq_spec(grp, 1, D_MODEL)]
    args = [h, w_in, w_in, w_in, w_in, w_branch, conv_w, conv_b, wa, ba, wi, bi, sp, conv0, h0,
            oa, oc, x, w_out, gate]
    out_specs = [_seq_spec(grp, LRU_CONV - 1, LRU_WIDTH), _seq_spec(grp, 1, LRU_WIDTH)]
    out_shape = [jax.ShapeDtypeStruct((grp.batch, LRU_CONV - 1, LRU_WIDTH), F32),
                 jax.ShapeDtypeStruct((grp.batch, 1, LRU_WIDTH), F32)]
    if last:
        in_specs.append(pl.BlockSpec((1, D_MODEL), lambda b, j: (0, 0)))
        args.append(norm_g)
        out_specs.append(tok)
        out_shape.append(jax.ShapeDtypeStruct(x.shape, F32))
    else:
        in_specs += [_layer_spec(layer + 1, 1, D_MODEL), _seq_spec(grp, 1, D_MODEL), _seq_spec(grp, 1, D_MODEL)]
        args += [norm_g, scale, shift]
        out_specs += [tok, tok]
        out_shape += [jax.ShapeDtypeStruct(x.shape, F32), jax.ShapeDtypeStruct(x.shape, BF16)]
    resident = (4 * D_MODEL * COL + COL * D_MODEL + D_MODEL * D_MODEL + 2 * LRU_WIDTH * LRU_WIDTH // LRU_BLOCKS) * 2
    streamed = 2 * tm * D_MODEL * (3 * 2 + 2 * 4 + 2)
    scratch = grp.nb * SUBLANES * LRU_WIDTH * 4
    temps = 8 * tm * D_MODEL * 4
    return pl.pallas_call(
        functools.partial(_rglru_kernel, grp=grp, last=last),
        grid=grp.grid,
        in_specs=in_specs,
        out_specs=out_specs,
        out_shape=out_shape,
        scratch_shapes=[pltpu.VMEM((grp.nb, SUBLANES, LRU_WIDTH), F32),
                        pltpu.VMEM((grp.nb, 1, LRU_WIDTH), F32)],
        compiler_params=_params(resident + streamed + scratch + temps),
        name="rglru_merge",
    )(*args)


def _gmlp_mix_weights(grp, ws, bs):
    L = grp.gm_chunk
    reps = grp.tm // L
    w = jnp.where(jnp.tril(jnp.ones((L, L), dtype=bool))[None], ws[:, :L, :L], 0.0)
    wbd = jnp.einsum('ab,gts->gatbs', jnp.eye(reps, dtype=F32), w).reshape(GM_GROUPS, grp.tm, grp.tm)
    bias = jnp.repeat(jnp.tile(bs[:, :L].T, (reps, 1)), GM_WIDTH // GM_GROUPS, axis=1)
    return wbd.astype(BF16), bias


def _run_group(grp, x, mod, conv0, h0, s0, want_v, weights):
    (norm_g, w_in, gm_vnorm_g, gm_ws, gm_bs, conv_w, conv_b, wa, ba, wi, bi, sp,
     loglb, log1mlb, omlb, onorm_g, w_branch, w_out, final_g) = weights
    shift = mod[:, :, None, 0:D_MODEL]
    scale = mod[:, :, None, D_MODEL:2 * D_MODEL]
    gate = mod[:, :, None, 2 * D_MODEL:]
    h = _prenorm_call(_Group(grp.batch, grp.seq, NORM_TILE_ROWS), 0, x, norm_g, scale[0], shift[0])
    hg_grp = grp if grp.nb == 1 else _Group(grp.batch, grp.seq, HG_MULTISEQ_ROWS)
    convs, hs, ss, vs = [], [], [], []
    y = None
    for l in range(DEPTH):
        wbd, bias = _gmlp_mix_weights(grp, gm_ws[l], gm_bs[l])
        res = _gmlp_call(grp, l, h, w_in, w_branch, gm_vnorm_g, wbd, bias, want_v)
        oa = res[0]
        if want_v:
            vs.append(res[1])
        oc, s_n = _hgrn2_call(hg_grp, l, h, w_in, w_branch, loglb, log1mlb, omlb, onorm_g, s0)
        lru = (grp, l, h, w_in, w_branch, w_out, conv_w, conv_b, wa, ba, wi, bi, sp, conv0, h0, oa, oc, x, gate[l])
        if l + 1 < DEPTH:
            conv_n, h_n, x, h = _rglru_call(*lru, norm_g, scale[l + 1], shift[l + 1])
        else:
            conv_n, h_n, y = _rglru_call(*lru, final_g)
        convs.append(conv_n)
        hs.append(h_n[:, 0, :])
        ss.append(s_n)
    return y, jnp.stack(convs), jnp.stack(hs), jnp.stack(ss), (jnp.stack(vs) if want_v else None)


def kernel(x_prompt, x_sample, c_prompt, c_sample, state_rglru_conv, state_rglru_h, state_hgrn2,
           w_ada, b_ada, norm_g, w_in, gm_vnorm_g, gm_ws, gm_bs, lru_conv_w, lru_conv_b,
           lru_wa, lru_ba, lru_wx, lru_bx, lru_lambda, hg_lb, hg_onorm_g, w_branch, w_out, final_g):
    batch_p, seq_p, _ = x_prompt.shape
    batch_s, seq_s, _ = x_sample.shape
    grp_p, grp_s = _Group(batch_p, seq_p), _Group(batch_s, seq_s)

    p = jax.nn.softmax(hg_lb.astype(F32), axis=0)
    cs = jnp.cumsum(p, axis=0)
    lbs = cs - cs[0]
    row = lambda a: a.reshape(DEPTH, 1, a.shape[-1])
    weights = (row(norm_g), w_in.astype(BF16), row(gm_vnorm_g), gm_ws, gm_bs,
               lru_conv_w, row(lru_conv_b), lru_wa.astype(BF16), row(lru_ba), lru_wx.astype(BF16), row(lru_bx),
               row(jax.nn.softplus(-lru_lambda.astype(F32))),
               row(jnp.log(lbs)), row(jnp.log1p(-lbs)), row(1.0 - lbs), row(hg_onorm_g),
               w_branch.astype(BF16), w_out.astype(BF16), final_g.reshape(1, D_MODEL))

    mod = _mod_call(jnp.concatenate([c_prompt, c_sample], axis=0), w_ada, b_ada)
    dt = x_prompt.dtype
    y_p, conv_p, h_p, s_p, _ = _run_group(
        grp_p, x_prompt, mod[:, :batch_p],
        jnp.zeros((DEPTH, batch_p, LRU_CONV - 1, LRU_WIDTH), dt),
        jnp.zeros((DEPTH, batch_p, 1, LRU_WIDTH), dt),
        jnp.zeros((DEPTH, batch_p, HG_HEADS, HG_DK, HG_DV), dt), False, weights)
    y_s, conv_s, h_s, s_s, v_s = _run_group(
        grp_s, x_sample, mod[:, batch_p:],
        state_rglru_conv, state_rglru_h[:, :, None, :], state_hgrn2, True, weights)
    return (y_p, y_s, conv_p, h_p, s_p, conv_s, h_s, s_s, v_s)
```

```python
import functools

import jax
import jax.numpy as jnp
from jax import lax
from jax.experimental import pallas as pl
from jax.experimental.pallas import tpu as pltpu

F32 = jnp.float32
BF16 = jnp.bfloat16

D_MODEL = 2048
DEPTH = 4
GM_CHUNK = 128
GM_GROUPS = 8
GM_WIDTH = 1024
LRU_WIDTH = 1024
LRU_BLOCKS = 8
LRU_CONV = 4
LRU_C = 8.0
HG_CHUNK = 64
HG_HEADS = 8
HG_DK = 128
HG_DV = 128
HG_WIDTH = HG_HEADS * HG_DV
EPS = 1e-6

LANES = 128
SUBLANES = 8
COL = 1024
COL_GM_U, COL_GM_V, COL_GM_G = 0, 1, 2
COL_LRU_X, COL_LRU_G = 3, 4
COL_HG_Q, COL_HG_F, COL_HG_I, COL_HG_O = 5, 6, 7, 8
COL_MERGE = 9
TILE_ROWS = 256
NORM_TILE_ROWS = 1024
PROJ_TILE_ROWS = 512
HG_SUB = 16
HG_CLAMP = 60.0
V7X_VMEM_BYTES = 64 * 1024 * 1024


class _Group:
    def __init__(self, batch, seq, tile_rows=TILE_ROWS):
        self.batch, self.seq = batch, seq
        self.ts = min(seq, tile_rows)
        self.nb = min(tile_rows // self.ts, batch)
        assert batch % self.nb == 0 and seq % self.ts == 0
        self.tm = self.nb * self.ts
        self.grid = (batch // self.nb, seq // self.ts)
        self.gm_chunk = min(seq, GM_CHUNK)
        self.hg_chunk = min(seq, HG_CHUNK)
        assert self.ts % self.gm_chunk == 0 and self.ts % self.hg_chunk == 0
        assert self.hg_chunk % HG_SUB == 0
        assert self.nb == 1 or self.grid[1] == 1


def _dot(a, b):
    return jnp.dot(a, b, preferred_element_type=F32)


def _rmsnorm(x, g):
    return x * lax.rsqrt(jnp.mean(x * x, axis=-1, keepdims=True) + EPS) * g


def _adaln(x, g, scale, shift):
    return (_rmsnorm(x, g) * (1.0 + scale) + shift).astype(BF16)


def _tok_spec(grp, width):
    return pl.BlockSpec((grp.nb, grp.ts, width), lambda b, j: (b, j, 0))


def _seq_spec(grp, *tail):
    zeros = (0,) * len(tail)
    return pl.BlockSpec((grp.nb,) + tail, lambda b, j: (b,) + zeros)


def _layer_seq_spec(grp, layer, *tail, **kwargs):
    zeros = (0,) * len(tail)
    return pl.BlockSpec((None, grp.nb) + tail, lambda b, j: (layer, b) + zeros, **kwargs)


def _const_spec(shape):
    nd = len(shape)
    return pl.BlockSpec(shape, lambda b, j: (0,) * nd, pipeline_mode=pl.Buffered(1))


def _layer_spec(layer, *tail):
    zeros = (0,) * len(tail)
    return pl.BlockSpec((None,) + tail, lambda b, j: (layer,) + zeros, pipeline_mode=pl.Buffered(1))


def _win_spec(layer, col):
    return pl.BlockSpec((None, D_MODEL, COL), lambda b, j: (layer, 0, col), pipeline_mode=pl.Buffered(1))


def _wbr_spec(layer, branch):
    return pl.BlockSpec((None, COL, D_MODEL), lambda b, j: (layer, branch, 0), pipeline_mode=pl.Buffered(1))


def _params(vmem_bytes):
    return pltpu.CompilerParams(dimension_semantics=("arbitrary", "arbitrary"),
                                vmem_limit_bytes=min(int(vmem_bytes), V7X_VMEM_BYTES - (4 << 20)))


def _merge_gates(hh, mg0_ref, mg1_ref):
    return [jax.nn.sigmoid(_dot(hh, mg_ref[...])) for mg_ref in (mg0_ref, mg1_ref)]


def _gated_out(gates, y, wbr_ref):
    return jnp.concatenate(
        [g * _dot(y, wbr_ref[:, half * COL:(half + 1) * COL]) for half, g in enumerate(gates)], axis=-1)


def _scan_rows(a, b, seg):
    rows, width = b.shape
    groups = rows // SUBLANES
    b3 = b.reshape(groups, SUBLANES, width)
    a3 = None if a is None else a.reshape(groups, SUBLANES, width)
    pos = lax.broadcasted_iota(jnp.int32, (1, SUBLANES, 1), 1)
    d = 1
    while d < SUBLANES:
        keep = pos >= d
        b_prev = pltpu.roll(b3, d, 1)
        if a3 is None:
            b3 = jnp.where(keep, b3 + b_prev, b3)
        else:
            b3 = jnp.where(keep, a3 * b_prev + b3, b3)
            a3 = jnp.where(keep, a3 * pltpu.roll(a3, d, 1), a3)
        d *= 2
    per_seg = seg // SUBLANES
    out = []
    for g in range(groups):
        hg = b3[g]
        if g % per_seg:
            carry = out[-1][SUBLANES - 1:SUBLANES, :]
            hg = hg + carry if a3 is None else a3[g] * carry + hg
        out.append(hg)
    return jnp.concatenate(out, axis=0)


def _mod_kernel(c_ref, w_ref, b_ref, o_ref):
    sc = jax.nn.silu(c_ref[...]).astype(BF16)
    o_ref[...] = _dot(sc, w_ref[...].astype(BF16)) + b_ref[...]


def _mod_call(c_all, w_ada, b_ada):
    rows = c_all.shape[0]
    width = 3 * D_MODEL
    tn = 1024
    return pl.pallas_call(
        _mod_kernel,
        grid=(DEPTH, width // tn),
        in_specs=[pl.BlockSpec((rows, D_MODEL), lambda l, n: (0, 0)),
                  pl.BlockSpec((None, D_MODEL, tn), lambda l, n: (l, 0, n)),
                  pl.BlockSpec((None, 1, tn), lambda l, n: (l, 0, n))],
        out_specs=pl.BlockSpec((None, rows, tn), lambda l, n: (l, 0, n)),
        out_shape=jax.ShapeDtypeStruct((DEPTH, rows, width), F32),
        compiler_params=_params(2 * D_MODEL * tn * 4 + D_MODEL * tn * 2 + (8 << 20)),
        name="adaln_mod",
    )(c_all, w_ada, b_ada.reshape(DEPTH, 1, width))


def _prenorm_kernel(x_ref, g_ref, scale_ref, shift_ref, h_ref):
    h_ref[...] = _adaln(x_ref[...], g_ref[...], scale_ref[...], shift_ref[...])


def _prenorm_call(grp, layer, x, norm_g, scale, shift):
    return pl.pallas_call(
        _prenorm_kernel,
        grid=grp.grid,
        in_specs=[_tok_spec(grp, D_MODEL), _layer_spec(layer, 1, D_MODEL),
                  _seq_spec(grp, 1, D_MODEL), _seq_spec(grp, 1, D_MODEL)],
        out_specs=_tok_spec(grp, D_MODEL),
        out_shape=jax.ShapeDtypeStruct(x.shape, BF16),
        compiler_params=_params(grp.tm * D_MODEL * (2 * 4 + 2 * 2 + 3 * 4) + (4 << 20)),
        name="prenorm",
    )(x, norm_g, scale, shift)


def _gmlp_kernel(h_ref, wu_ref, wv_ref, wg_ref, mg0_ref, mg1_ref, wbr_ref, vg_ref, wbd_ref, bias_ref,
                 out_ref, *v_out, grp):
    hh = h_ref[...].reshape(grp.tm, D_MODEL)
    zv = _dot(hh, wv_ref[...])
    zu = _dot(hh, wu_ref[...])
    zg = _dot(hh, wg_ref[...])
    merge_gates = _merge_gates(hh, mg0_ref, mg1_ref)
    v = _rmsnorm(jax.nn.gelu(zv), vg_ref[...])
    if v_out:
        v_out[0][...] = v.reshape(grp.nb, grp.ts, GM_WIDTH)
    vb = v.astype(BF16)
    width = GM_WIDTH // GM_GROUPS
    s = jnp.concatenate([_dot(wbd_ref[g], vb[:, g * width:(g + 1) * width]) for g in range(GM_GROUPS)],
                        axis=-1) + bias_ref[...]
    ya = (jax.nn.gelu(zu) * s) * jax.nn.silu(zg)
    oa = _gated_out(merge_gates, ya.astype(BF16), wbr_ref)
    out_ref[...] = oa.astype(BF16).reshape(grp.nb, grp.ts, D_MODEL)


def _gmlp_call(grp, layer, h, w_in, w_branch, vnorm_g, wbd, bias, want_v):
    tm = grp.tm
    out_shape = [jax.ShapeDtypeStruct((grp.batch, grp.seq, D_MODEL), BF16)]
    out_specs = [_tok_spec(grp, D_MODEL)]
    if want_v:
        out_shape.append(jax.ShapeDtypeStruct((grp.batch, grp.seq, GM_WIDTH), F32))
        out_specs.append(_tok_spec(grp, GM_WIDTH))
    resident = (5 * D_MODEL * COL + COL * D_MODEL + GM_GROUPS * tm * tm) * 2 + tm * GM_WIDTH * 4
    streamed = 2 * tm * (D_MODEL * 2 + D_MODEL * 2 + GM_WIDTH * 4)
    temps = 12 * tm * GM_WIDTH * 4
    return pl.pallas_call(
        functools.partial(_gmlp_kernel, grp=grp),
        grid=grp.grid,
        in_specs=[_tok_spec(grp, D_MODEL),
                  _win_spec(layer, COL_GM_U), _win_spec(layer, COL_GM_V), _win_spec(layer, COL_GM_G),
                  _win_spec(layer, COL_MERGE), _win_spec(layer, COL_MERGE + 1), _wbr_spec(layer, 0),
                  _layer_spec(layer, 1, GM_WIDTH), _const_spec(wbd.shape), _const_spec(bias.shape)],
        out_specs=out_specs,
        out_shape=out_shape,
        compiler_params=_params(resident + streamed + temps),
        name="gmlp_branch",
    )(h, w_in, w_in, w_in, w_in, w_in, w_branch, vnorm_g, wbd, bias)


def _hgrn2_kernel(h_ref, wq_ref, wf_ref, wi_ref, wo_ref,
                  loglb_ref, log1mlb_ref, omlb_ref, og_ref, s0_ref,
                  out_ref, sn_ref,
                  qe_scr, qt_scr, kd_scr, kt_scr, v_scr, inter_scr, intra_scr, cq_scr, ck_scr, cv_scr, cg_scr,
                  *, grp):
    nb, tm, L = grp.nb, grp.tm, grp.hg_chunk
    n_chunks = tm // L
    n_sub = L // HG_SUB
    W = HG_WIDTH

    @pl.when(pl.program_id(1) == 0)
    def _():
        sn_ref[...] = s0_ref[...]

    hh = h_ref[...].reshape(tm, D_MODEL)

    def gates(rows_h):
        q = jax.nn.silu(_dot(rows_h, wq_ref[...]))
        zf = _dot(rows_h, wf_ref[...])
        magnitude = jnp.abs(zf)
        e = jnp.exp(-magnitude)
        r = 1.0 / (1.0 + e)
        k = omlb_ref[...] * jnp.where(zf >= 0.0, e * r, r)
        log_sigmoid = jnp.minimum(zf, 0.0) + jnp.log(r)
        a, b = loglb_ref[...], log1mlb_ref[...] + log_sigmoid
        log_f = jnp.maximum(a, b) + jnp.log(1.0 + jnp.exp(-jnp.abs(a - b)))
        return q, k, log_f, _dot(rows_h, wi_ref[...])

    q, k, log_f, v = gates(hh)
    shape4 = (n_chunks, n_sub, HG_SUB, W)
    g_sub = _scan_rows(None, log_f, HG_SUB).reshape(shape4)
    totals = g_sub[:, :, HG_SUB - 1:HG_SUB, :]
    refs = []
    acc = jnp.zeros((n_chunks, 1, 1, W), F32)
    for i in range(n_sub):
        refs.append(acc)
        acc = acc + totals[:, i:i + 1]
    g_last = acc
    G = g_sub + jnp.concatenate(refs, axis=1)
    q4, k4 = q.reshape(shape4), k.reshape(shape4)
    operand = lambda a: a.reshape(tm, W).astype(BF16)
    qe_scr[...] = operand(q4 * jnp.exp(G))
    qt_scr[...] = operand(q4 * jnp.exp(g_sub))
    kd_scr[...] = operand(k4 * jnp.exp(g_last - G))
    for i in range(n_sub):
        kt_scr[i] = operand(k4 * jnp.exp(jnp.minimum(refs[i] - G, HG_CLAMP)))
    v_scr[...] = v.astype(BF16)
    chunk_decay = jnp.exp(g_last).reshape(n_chunks, W)
    out_gate = jax.nn.silu(_dot(hh, wo_ref[...]))

    tril = lax.broadcasted_iota(jnp.int32, (L, L), 0) >= lax.broadcasted_iota(jnp.int32, (L, L), 1)
    heads = range(HG_HEADS)
    head_cols = [slice(hd * HG_DK, (hd + 1) * HG_DK) for hd in heads]
    chunk_rows = [slice(c * L, (c + 1) * L) for c in range(n_chunks)]
    contract_last = (((1,), (1,)), ((), ()))
    contract_rows = (((0,), (0,)), ((), ()))

    def score_rows(c, i, cols):
        seen = (i + 1) * HG_SUB
        a = lax.dot_general(qt_scr[c * L + i * HG_SUB:c * L + seen, cols], kt_scr[i, c * L:c * L + seen, cols],
                            contract_last, preferred_element_type=F32)
        return a if seen == L else jnp.concatenate([a, jnp.zeros((HG_SUB, L - seen), F32)], axis=1)

    for c, rows in enumerate(chunk_rows):
        scores = [jnp.concatenate([score_rows(c, i, cols) for i in range(n_sub)], axis=0) for cols in head_cols]
        scores = [jnp.where(tril, A, 0.0).astype(BF16) for A in scores]
        intra_scr[rows, :] = jnp.concatenate([_dot(A, v_scr[rows, cols]) for A, cols in zip(scores, head_cols)], axis=-1)
    updates = [[lax.dot_general(kd_scr[rows, cols], v_scr[rows, cols], contract_rows, preferred_element_type=F32)
                for cols in head_cols] for rows in chunk_rows]
    S = None
    for c, rows in enumerate(chunk_rows):
        slot = 0 if nb == 1 else c
        if nb > 1 or c == 0:
            S = [sn_ref[slot, hd] for hd in heads]
        inter_scr[rows, :] = jnp.concatenate(
            [_dot(qe_scr[rows, cols], S[hd].astype(BF16)) for hd, cols in enumerate(head_cols)], axis=-1)
        S = [jnp.transpose(jnp.broadcast_to(chunk_decay[c:c + 1, cols], (HG_DV, HG_DK))) * S[hd] + updates[c][hd]
             for hd, cols in enumerate(head_cols)]
        if nb > 1 or c == n_chunks - 1:
            for hd in heads:
                sn_ref[slot, hd] = S[hd]

    @pl.when(jnp.min(totals) < -HG_CLAMP)
    def _():
        t_idx = lax.broadcasted_iota(jnp.int32, (L, 1), 0)
        for c in range(n_chunks):
            rows = slice(c * L, (c + 1) * L)
            cq_scr[...], ck_scr[...], log_f_c, cv_scr[...] = gates(hh[rows])
            cg_scr[...] = _scan_rows(None, log_f_c, L)
            for hd in range(HG_HEADS):
                cols = slice(hd * HG_DK, (hd + 1) * HG_DK)
                G_c, q_c = cg_scr[:, cols], cq_scr[:, cols]

                def add_sources(blk, acc):
                    src = pl.ds(pl.multiple_of(blk * SUBLANES, SUBLANES), SUBLANES)
                    g_blk, k_blk, v_blk = cg_scr[src, cols], ck_scr[src, cols], cv_scr[src, cols]
                    for r in range(SUBLANES):
                        g_s, k_s, v_s = g_blk[r:r + 1, :], k_blk[r:r + 1, :], v_blk[r:r + 1, :]
                        w = jnp.sum(q_c * k_s * jnp.exp(jnp.minimum(G_c - g_s, 0.0)), axis=-1, keepdims=True)
                        acc = acc + jnp.where(t_idx >= blk * SUBLANES + r, w, 0.0) * v_s
                    return acc

                intra_scr[rows, cols] = lax.fori_loop(0, L // SUBLANES, add_sources, jnp.zeros((L, HG_DV), F32))

    o = inter_scr[...] + intra_scr[...]
    y = jnp.concatenate(
        [_rmsnorm(o[:, hd * HG_DV:(hd + 1) * HG_DV], og_ref[...]) for hd in range(HG_HEADS)], axis=-1)
    out_ref[...] = (y * out_gate).astype(BF16).reshape(grp.nb, grp.ts, HG_WIDTH)


def _hgrn2_call(grp, layer, h, w_in, loglb, log1mlb, omlb, onorm_g, s0):
    tm, L = grp.tm, grp.hg_chunk
    n_sub = L // HG_SUB
    state = grp.nb * HG_HEADS * HG_DK * HG_DV * 4
    resident = 4 * D_MODEL * COL * 2
    streamed = 2 * tm * (D_MODEL * 2 + D_MODEL * 2) + 3 * state
    scratch = (4 + n_sub) * tm * HG_WIDTH * 2 + 2 * tm * HG_WIDTH * 4 + 4 * L * HG_WIDTH * 4
    temps = 10 * tm * HG_WIDTH * 4
    tile16 = pltpu.VMEM((tm, HG_WIDTH), BF16)
    tile32 = pltpu.VMEM((tm, HG_WIDTH), F32)
    chunk32 = pltpu.VMEM((L, HG_WIDTH), F32)
    return pl.pallas_call(
        functools.partial(_hgrn2_kernel, grp=grp),
        grid=grp.grid,
        in_specs=[_tok_spec(grp, D_MODEL),
                  _win_spec(layer, COL_HG_Q), _win_spec(layer, COL_HG_F), _win_spec(layer, COL_HG_I),
                  _win_spec(layer, COL_HG_O),
                  _layer_spec(layer, 1, HG_WIDTH), _layer_spec(layer, 1, HG_WIDTH), _layer_spec(layer, 1, HG_WIDTH),
                  _layer_spec(layer, 1, HG_DV),
                  _layer_seq_spec(grp, layer, HG_HEADS, HG_DK, HG_DV, pipeline_mode=pl.Buffered(1))],
        out_specs=[_tok_spec(grp, HG_WIDTH), _seq_spec(grp, HG_HEADS, HG_DK, HG_DV)],
        out_shape=[jax.ShapeDtypeStruct((grp.batch, grp.seq, HG_WIDTH), BF16),
                   jax.ShapeDtypeStruct((grp.batch, HG_HEADS, HG_DK, HG_DV), F32)],
        scratch_shapes=[tile16, tile16, tile16, pltpu.VMEM((n_sub, tm, HG_WIDTH), BF16), tile16,
                        tile32, tile32, chunk32, chunk32, chunk32, chunk32],
        compiler_params=_params(resident + streamed + scratch + temps),
        name="hgrn2_branch",
    )(h, w_in, w_in, w_in, w_in, loglb, log1mlb, omlb, onorm_g, s0)


def _gated_proj_kernel(h_ref, y_ref, mg0_ref, mg1_ref, wbr_ref, out_ref, *, grp):
    hh = h_ref[...].reshape(grp.tm, D_MODEL)
    y = y_ref[...].reshape(grp.tm, COL)
    out = _gated_out(_merge_gates(hh, mg0_ref, mg1_ref), y, wbr_ref)
    out_ref[...] = out.astype(BF16).reshape(grp.nb, grp.ts, D_MODEL)


def _gated_proj_call(grp, layer, branch, h, y, w_in, w_branch):
    tm = grp.tm
    resident = (2 * D_MODEL * COL + COL * D_MODEL) * 2
    streamed = 2 * tm * (D_MODEL * 2 + COL * 2 + D_MODEL * 2)
    temps = 6 * tm * D_MODEL * 4
    return pl.pallas_call(
        functools.partial(_gated_proj_kernel, grp=grp),
        grid=grp.grid,
        in_specs=[_tok_spec(grp, D_MODEL), _tok_spec(grp, COL),
                  _win_spec(layer, COL_MERGE + 2 * branch), _win_spec(layer, COL_MERGE + 2 * branch + 1),
                  _wbr_spec(layer, branch)],
        out_specs=_tok_spec(grp, D_MODEL),
        out_shape=jax.ShapeDtypeStruct((grp.batch, grp.seq, D_MODEL), BF16),
        compiler_params=_params(resident + streamed + temps),
        name="gated_proj",
    )(h, y, w_in, w_in, w_branch)


def _rglru_kernel(h_ref, wx_ref, wg_ref, mg0_ref, mg1_ref, wbr_ref, convw_ref, convb_ref, wa_ref, ba_ref,
                  wi_ref, bi_ref, sp_ref, conv0_ref, h0_ref, oa_ref, oc_ref, x_ref, wout_ref, gate_ref, g_ref,
                  *rest, grp, last):
    if last:
        convn_ref, hn_ref, y_ref, tail_scr, hcar = rest
    else:
        scale_ref, shift_ref, convn_ref, hn_ref, xn_ref, hnext_ref, tail_scr, hcar = rest
    nb, ts, tm = grp.nb, grp.ts, grp.tm
    groups = ts // SUBLANES
    history = SUBLANES - (LRU_CONV - 1)

    @pl.when(pl.program_id(1) == 0)
    def _():
        tail_scr[...] = jnp.zeros(tail_scr.shape, F32)
        tail_scr[:, history:, :] = conv0_ref[...]
        hcar[...] = h0_ref[...]

    hh = h_ref[...].reshape(tm, D_MODEL)
    lx = _dot(hh, wx_ref[...]).reshape(nb, groups, SUBLANES, LRU_WIDTH)
    ext = jnp.concatenate([tail_scr[...].reshape(nb, 1, SUBLANES, LRU_WIDTH), lx], axis=1)
    pos = lax.broadcasted_iota(jnp.int32, (1, 1, SUBLANES, 1), 2)
    y = convb_ref[...]
    for tap in range(LRU_CONV):
        shift = LRU_CONV - 1 - tap
        if shift:
            rolled = pltpu.roll(ext.reshape(nb * (groups + 1), SUBLANES, LRU_WIDTH), shift, 1)
            rolled = rolled.reshape(nb, groups + 1, SUBLANES, LRU_WIDTH)
            shifted = jnp.where(pos >= shift, rolled[:, 1:], rolled[:, :groups])
        else:
            shifted = lx
        y = y + convw_ref[tap:tap + 1, :] * shifted
    tail = lx[:, groups - 1]
    tail_scr[...] = tail
    convn_ref[...] = tail[:, history:, :]

    x = y.reshape(tm, LRU_WIDTH)
    xb = x.astype(BF16)
    width = LRU_WIDTH // LRU_BLOCKS

    def block_diag(w_ref, b_ref):
        return jnp.concatenate([_dot(xb[:, n * width:(n + 1) * width], w_ref[n]) for n in range(LRU_BLOCKS)],
                               axis=-1) + b_ref[...]

    gate_r = jax.nn.sigmoid(block_diag(wa_ref, ba_ref))
    gate_i = jax.nn.sigmoid(block_diag(wi_ref, bi_ref))
    log_a = -LRU_C * gate_r * sp_ref[...]
    a = jnp.exp(log_a)
    drive = jnp.sqrt(-jnp.tanh(log_a) * (a * a + 1.0)) * gate_i * x
    h0 = jnp.broadcast_to(hcar[...], (nb, ts, LRU_WIDTH)).reshape(tm, LRU_WIDTH)
    first = lax.broadcasted_iota(jnp.int32, (tm, 1), 0) % ts == 0
    drive = drive + jnp.where(first, a * h0, 0.0)
    hseq = _scan_rows(a, drive, ts)
    h_last = hseq.reshape(nb, ts, LRU_WIDTH)[:, ts - 1:ts, :]
    hcar[...] = h_last
    hn_ref[...] = h_last
    yb = hseq * jax.nn.silu(_dot(hh, wg_ref[...]))
    ob = _gated_out(_merge_gates(hh, mg0_ref, mg1_ref), yb.astype(BF16), wbr_ref)

    as_rows = lambda ref: ref[...].reshape(tm, D_MODEL).astype(F32)
    merged = (as_rows(oa_ref) + ob) + as_rows(oc_ref)
    out = _dot(merged.astype(BF16), wout_ref[...]).reshape(nb, ts, D_MODEL)
    xn = x_ref[...] + gate_ref[...] * out
    if last:
        y_ref[...] = _rmsnorm(xn, g_ref[...])
    else:
        xn_ref[...] = xn
        hnext_ref[...] = _adaln(xn, g_ref[...], scale_ref[...], shift_ref[...])


def _rglru_call(grp, layer, h, w_in, w_branch, w_out, conv_w, conv_b, wa, ba, wi, bi, sp, conv0, h0,
                oa, oc, x, gate, norm_g, scale=None, shift=None):
    last = scale is None
    tm = grp.tm
    tok = _tok_spec(grp, D_MODEL)
    in_specs = [tok,
                _win_spec(layer, COL_LRU_X), _win_spec(layer, COL_LRU_G),
                _win_spec(layer, COL_MERGE + 2), _win_spec(layer, COL_MERGE + 3), _wbr_spec(layer, 1),
                _layer_spec(layer, LRU_CONV, LRU_WIDTH), _layer_spec(layer, 1, LRU_WIDTH),
                _layer_spec(layer, LRU_BLOCKS, LANES, LANES), _layer_spec(layer, 1, LRU_WIDTH),
                _layer_spec(layer, LRU_BLOCKS, LANES, LANES), _layer_spec(layer, 1, LRU_WIDTH),
                _layer_spec(layer, 1, LRU_WIDTH),
                _layer_seq_spec(grp, layer, LRU_CONV - 1, LRU_WIDTH), _layer_seq_spec(grp, layer, 1, LRU_WIDTH),
                tok, tok, tok, _layer_spec(layer, D_MODEL, D_MODEL), _seq_spec(grp, 1, D_MODEL)]
    args = [h, w_in, w_in, w_in, w_in, w_branch, conv_w, conv_b, wa, ba, wi, bi, sp, conv0, h0,
            oa, oc, x, w_out, gate]
    out_specs = [_seq_spec(grp, LRU_CONV - 1, LRU_WIDTH), _seq_spec(grp, 1, LRU_WIDTH)]
    out_shape = [jax.ShapeDtypeStruct((grp.batch, LRU_CONV - 1, LRU_WIDTH), F32),
                 jax.ShapeDtypeStruct((grp.batch, 1, LRU_WIDTH), F32)]
    if last:
        in_specs.append(pl.BlockSpec((1, D_MODEL), lambda b, j: (0, 0)))
        args.append(norm_g)
        out_specs.append(tok)
        out_shape.append(jax.ShapeDtypeStruct(x.shape, F32))
    else:
        in_specs += [_layer_spec(layer + 1, 1, D_MODEL), _seq_spec(grp, 1, D_MODEL), _seq_spec(grp, 1, D_MODEL)]
        args += [norm_g, scale, shift]
        out_specs += [tok, tok]
        out_shape += [jax.ShapeDtypeStruct(x.shape, F32), jax.ShapeDtypeStruct(x.shape, BF16)]
    resident = (4 * D_MODEL * COL + COL * D_MODEL + D_MODEL * D_MODEL + 2 * LRU_WIDTH * LRU_WIDTH // LRU_BLOCKS) * 2
    streamed = 2 * tm * D_MODEL * (3 * 2 + 2 * 4 + 2)
    scratch = grp.nb * SUBLANES * LRU_WIDTH * 4
    temps = 8 * tm * D_MODEL * 4
    return pl.pallas_call(
        functools.partial(_rglru_kernel, grp=grp, last=last),
        grid=grp.grid,
        in_specs=in_specs,
        out_specs=out_specs,
        out_shape=out_shape,
        scratch_shapes=[pltpu.VMEM((grp.nb, SUBLANES, LRU_WIDTH), F32),
                        pltpu.VMEM((grp.nb, 1, LRU_WIDTH), F32)],
        compiler_params=_params(resident + streamed + scratch + temps),
        name="rglru_merge",
    )(*args)


def _gmlp_mix_weights(grp, ws, bs):
    L = grp.gm_chunk
    reps = grp.tm // L
    w = jnp.where(jnp.tril(jnp.ones((L, L), dtype=bool))[None], ws[:, :L, :L], 0.0)
    wbd = jnp.einsum('ab,gts->gatbs', jnp.eye(reps, dtype=F32), w).reshape(GM_GROUPS, grp.tm, grp.tm)
    bias = jnp.repeat(jnp.tile(bs[:, :L].T, (reps, 1)), GM_WIDTH // GM_GROUPS, axis=1)
    return wbd.astype(BF16), bias


def _run_group(grp, x, mod, conv0, h0, s0, want_v, weights):
    (norm_g, w_in, gm_vnorm_g, gm_ws, gm_bs, conv_w, conv_b, wa, ba, wi, bi, sp,
     loglb, log1mlb, omlb, onorm_g, w_branch, w_out, final_g) = weights
    shift = mod[:, :, None, 0:D_MODEL]
    scale = mod[:, :, None, D_MODEL:2 * D_MODEL]
    gate = mod[:, :, None, 2 * D_MODEL:]
    h = _prenorm_call(_Group(grp.batch, grp.seq, NORM_TILE_ROWS), 0, x, norm_g, scale[0], shift[0])
    proj_grp = _Group(grp.batch, grp.seq, PROJ_TILE_ROWS) if grp.nb == 1 else grp
    convs, hs, ss, vs = [], [], [], []
    y = None
    for l in range(DEPTH):
        wbd, bias = _gmlp_mix_weights(grp, gm_ws[l], gm_bs[l])
        res = _gmlp_call(grp, l, h, w_in, w_branch, gm_vnorm_g, wbd, bias, want_v)
        oa = res[0]
        if want_v:
            vs.append(res[1])
        yc, s_n = _hgrn2_call(grp, l, h, w_in, loglb, log1mlb, omlb, onorm_g, s0)
        oc = _gated_proj_call(proj_grp, l, 2, h, yc, w_in, w_branch)
        lru = (grp, l, h, w_in, w_branch, w_out, conv_w, conv_b, wa, ba, wi, bi, sp, conv0, h0, oa, oc, x, gate[l])
        if l + 1 < DEPTH:
            conv_n, h_n, x, h = _rglru_call(*lru, norm_g, scale[l + 1], shift[l + 1])
        else:
            conv_n, h_n, y = _rglru_call(*lru, final_g)
        convs.append(conv_n)
        hs.append(h_n[:, 0, :])
        ss.append(s_n)
    return y, jnp.stack(convs), jnp.stack(hs), jnp.stack(ss), (jnp.stack(vs) if want_v else None)


def kernel(x_prompt, x_sample, c_prompt, c_sample, state_rglru_conv, state_rglru_h, state_hgrn2,
           w_ada, b_ada, norm_g, w_in, gm_vnorm_g, gm_ws, gm_bs, lru_conv_w, lru_conv_b,
           lru_wa, lru_ba, lru_wx, lru_bx, lru_lambda, hg_lb, hg_onorm_g, w_branch, w_out, final_g):
    batch_p, seq_p, _ = x_prompt.shape
    batch_s, seq_s, _ = x_sample.shape
    grp_p, grp_s = _Group(batch_p, seq_p), _Group(batch_s, seq_s)

    p = jax.nn.softmax(hg_lb.astype(F32), axis=0)
    cs = jnp.cumsum(p, axis=0)
    lbs = cs - cs[0]
    row = lambda a: a.reshape(DEPTH, 1, a.shape[-1])
    weights = (row(norm_g), w_in.astype(BF16), row(gm_vnorm_g), gm_ws, gm_bs,
               lru_conv_w, row(lru_conv_b), lru_wa.astype(BF16), row(lru_ba), lru_wx.astype(BF16), row(lru_bx),
               row(jax.nn.softplus(-lru_lambda.astype(F32))),
               row(jnp.log(lbs)), row(jnp.log1p(-lbs)), row(1.0 - lbs), row(hg_onorm_g),
               w_branch.astype(BF16), w_out.astype(BF16), final_g.reshape(1, D_MODEL))

    mod = _mod_call(jnp.concatenate([c_prompt, c_sample], axis=0), w_ada, b_ada)
    dt = x_prompt.dtype
    y_p, conv_p, h_p, s_p, _ = _run_group(
        grp_p, x_prompt, mod[:, :batch_p],
        jnp.zeros((DEPTH, batch_p, LRU_CONV - 1, LRU_WIDTH), dt),
        jnp.zeros((DEPTH, batch_p, 1, LRU_WIDTH), dt),
        jnp.zeros((DEPTH, batch_p, HG_HEADS, HG_DK, HG_DV), dt), False, weights)
    y_s, conv_s, h_s, s_s, v_s = _run_group(
        grp_s, x_sample, mod[:, batch_p:],
        state_rglru_conv, state_rglru_h[:, :, None, :], state_hgrn2, True, weights)
    return (y_p, y_s, conv_p, h_p, s_p, conv_s, h_s, s_s, v_s)
```
